```python
import jax, jax.numpy as jnp
from jax import lax
import numpy as np

D_MODEL = 4096
BATCH = 2
SEQ = 8192
DEPTH = 4

HEAD_DIM = 128
D_MIX = D_MODEL
HG_HEADS = D_MODEL // 512
HG_K = 128
HG_V = HEAD_DIM
HG_QK = HG_HEADS * HG_K
HG_WIDTH = HG_HEADS * HG_V
HG_CHUNK = 64
FOX_HEADS = 3 * D_MODEL // 1024
FOX_WIDTH = FOX_HEADS * HEAD_DIM
MLA_HEADS = 3 * D_MODEL // 1024
MLA_NOPE = 128
MLA_ROPE = 64
MLA_V = 128
MLA_Q_RANK = D_MODEL // 4
MLA_KV_RANK = D_MODEL // 8
MLA_WIDTH = MLA_HEADS * MLA_V
ROPE_BASE = 10000.0
D_FF = ((8 * D_MODEL // 3 + 255) // 256) * 256
Q_BLOCK = 128
EPS = 1e-6
MASK_VALUE = -1e30
IN_SPLIT = (HG_QK, HG_QK, HG_WIDTH, HG_WIDTH,
            FOX_WIDTH, FOX_WIDTH, FOX_WIDTH, FOX_HEADS,
            MLA_Q_RANK, MLA_KV_RANK + MLA_ROPE)
IN_COLS = HG_QK * 2 + HG_WIDTH * 2 + FOX_WIDTH * 3 + FOX_HEADS + MLA_Q_RANK + MLA_KV_RANK + MLA_ROPE

kernel_name = "hymba_style_hgrn2_fox_mla_trunk"


def rms_norm(x, w):
    xf = x.astype(jnp.float32)
    y = xf * lax.rsqrt(jnp.mean(xf * xf, axis=-1, keepdims=True) + EPS)
    return (y * w.astype(jnp.float32)).astype(x.dtype)


def split_cols(t, sizes):
    offs = np.cumsum(np.array(sizes))[:-1].tolist()
    return jnp.split(t, offs, axis=-1)


def hgrn2_mixer(q_pre, f_pre, i_in, g_pre, lower_bound, out_norm_w):
    B, S, _ = q_pre.shape
    n_chunks = S // HG_CHUNK
    z = f_pre.astype(jnp.float32)
    lb = lower_bound.astype(jnp.float32)
    q = jax.nn.silu(q_pre.astype(jnp.float32)) * (HG_K ** -0.5)
    sig = jax.nn.sigmoid(z)
    k = (1.0 - lb) * jax.nn.sigmoid(-z)
    log_f = jnp.log(lb + (1.0 - lb) * sig)
    v = i_in.astype(jnp.float32)

    def chunks(t, d):
        return t.reshape(B, n_chunks, HG_CHUNK, HG_HEADS, d).transpose(1, 0, 3, 2, 4)

    causal = jnp.tril(jnp.ones((HG_CHUNK, HG_CHUNK), dtype=bool))[:, :, None]

    def step(state, inp):
        qc, kc, vc, lfc = inp
        b = jnp.cumsum(lfc, axis=2)
        diff = b[:, :, :, None, :] - b[:, :, None, :, :]
        decay = jnp.where(causal, jnp.exp(jnp.where(causal, diff, 0.0)), 0.0)
        scores = jnp.einsum('bhtsk,bhsk->bhts', qc[:, :, :, None, :] * decay, kc)
        o = (jnp.einsum('bhts,bhsv->bhtv', scores, vc)
             + jnp.einsum('bhtk,bhkv->bhtv', qc * jnp.exp(b), state))
        b_last = b[:, :, -1, :]
        new_state = (jnp.exp(b_last)[..., None] * state
                     + jnp.einsum('bhsk,bhsv->bhkv', kc * jnp.exp(b_last[:, :, None, :] - b), vc))
        return new_state, o

    state0 = jnp.zeros((B, HG_HEADS, HG_K, HG_V), jnp.float32)
    _, o = lax.scan(step, state0, (chunks(q, HG_K), chunks(k, HG_K), chunks(v, HG_V), chunks(log_f, HG_K)))
    o = o.transpose(1, 0, 3, 2, 4).reshape(B, S, HG_WIDTH)
    return (rms_norm(o, out_norm_w) * jax.nn.silu(g_pre.astype(jnp.float32))).astype(q_pre.dtype)


def causal_block_attention(q, k, v, scale, log_decay_cum=None):
    B, S, H, Dk = q.shape
    Dv = v.shape[-1]
    nb = S // Q_BLOCK
    qb = q.reshape(B, nb, Q_BLOCK, H, Dk).transpose(1, 0, 2, 3, 4)
    kf = k.astype(jnp.float32)
    kpos = jnp.arange(S)
    use_decay = log_decay_cum is not None
    if use_decay:
        c_keys = log_decay_cum.transpose(0, 2, 1)
        c_blocks = log_decay_cum.reshape(B, nb, Q_BLOCK, H).transpose(1, 0, 3, 2)
        xs = (jnp.arange(nb), qb, c_blocks)
    else:
        xs = (jnp.arange(nb), qb)

    def one_block(args):
        idx, q_blk = args[0], args[1]
        qpos = idx * Q_BLOCK + jnp.arange(Q_BLOCK)
        logits = jnp.einsum('bqhd,bkhd->bhqk', q_blk.astype(jnp.float32), kf) * scale
        mask = kpos[None, :] <= qpos[:, None]
        if use_decay:
            bias = args[2][..., None] - c_keys[:, :, None, :]
            logits = logits + jnp.where(mask, bias, 0.0)
        logits = jnp.where(mask, logits, MASK_VALUE)
        p = jax.nn.softmax(logits, axis=-1)
        return jnp.einsum('bhqk,bkhd->bqhd', p.astype(v.dtype), v)

    out = lax.map(one_block, xs)
    return out.transpose(1, 0, 2, 3, 4).reshape(B, S, H, Dv)


def fox_mixer(q, k, v, f_pre, f_bias):
    B, S, _ = q.shape
    log_f = jax.nn.log_sigmoid(f_pre.astype(jnp.float32) + f_bias.astype(jnp.float32))
    c = jnp.cumsum(log_f, axis=1)
    shp = (B, S, FOX_HEADS, HEAD_DIM)
    o = causal_block_attention(q.reshape(shp), k.reshape(shp), v.reshape(shp), HEAD_DIM ** -0.5, c)
    return o.reshape(B, S, FOX_WIDTH)


def apply_rope(t, cos, sin):
    t1, t2 = jnp.split(t.astype(jnp.float32), 2, axis=-1)
    return jnp.concatenate([t1 * cos - t2 * sin, t1 * sin + t2 * cos], axis=-1).astype(t.dtype)


def mla_mixer(q_a, kv_a, q_a_norm, w_q_b, kv_a_norm, w_kv_b, cos, sin):
    B, S, _ = q_a.shape
    q = (rms_norm(q_a, q_a_norm) @ w_q_b).reshape(B, S, MLA_HEADS, MLA_NOPE + MLA_ROPE)
    q_nope, q_pe = jnp.split(q, [MLA_NOPE], axis=-1)
    kv_lat, k_pe = jnp.split(kv_a, [MLA_KV_RANK], axis=-1)
    kv = (rms_norm(kv_lat, kv_a_norm) @ w_kv_b).reshape(B, S, MLA_HEADS, MLA_NOPE + MLA_V)
    k_nope, v = jnp.split(kv, [MLA_NOPE], axis=-1)
    q_pe = apply_rope(q_pe, cos, sin)
    k_pe = apply_rope(k_pe[:, :, None, :], cos, sin)
    q_full = jnp.concatenate([q_nope, q_pe], axis=-1)
    k_full = jnp.concatenate([k_nope, jnp.broadcast_to(k_pe, (B, S, MLA_HEADS, MLA_ROPE))], axis=-1)
    o = causal_block_attention(q_full, k_full, v, (MLA_NOPE + MLA_ROPE) ** -0.5)
    return o.reshape(B, S, MLA_WIDTH)


def setup_inputs(seed: int = 0) -> dict:
    key = jax.random.key(seed)
    ks = jax.random.split(key, 24)
    f32 = jnp.float32

    def dense(k, shape, fan_in):
        return jax.random.normal(k, shape, f32) * (fan_in ** -0.5)

    def gain(k, shape):
        return 1.0 + 0.02 * jax.random.normal(k, shape, f32)

    x = jax.random.normal(ks[0], (BATCH, SEQ, D_MODEL), f32)
    offset = jax.random.randint(ks[1], (BATCH, 1), 0, 4096, dtype=jnp.int32)
    positions = offset + jnp.arange(SEQ, dtype=jnp.int32)[None, :]
    return {
        "x": x,
        "positions": positions,
        "mix_norm": gain(ks[2], (DEPTH, D_MODEL)),
        "w_in": dense(ks[3], (DEPTH, D_MODEL, IN_COLS), D_MODEL),
        "hg_lower_bounds": 0.1 * jax.random.normal(ks[4], (DEPTH, HG_QK), f32),
        "hg_out_norm": gain(ks[5], (DEPTH, HG_WIDTH)),
        "fox_f_bias": jax.random.uniform(ks[6], (DEPTH, FOX_HEADS), f32, 1.0, 4.0),
        "fox_out_norm": gain(ks[7], (DEPTH, FOX_WIDTH)),
        "mla_q_a_norm": gain(ks[8], (DEPTH, MLA_Q_RANK)),
        "mla_w_q_b": dense(ks[9], (DEPTH, MLA_Q_RANK, MLA_HEADS * (MLA_NOPE + MLA_ROPE)), MLA_Q_RANK),
        "mla_kv_a_norm": gain(ks[10], (DEPTH, MLA_KV_RANK)),
        "mla_w_kv_b": dense(ks[11], (DEPTH, MLA_KV_RANK, MLA_HEADS * (MLA_NOPE + MLA_V)), MLA_KV_RANK),
        "mla_out_norm": gain(ks[12], (DEPTH, MLA_WIDTH)),
        "w_o": dense(ks[13], (DEPTH, D_MIX, D_MODEL), D_MIX),
        "ffn_norm": gain(ks[14], (DEPTH, D_MODEL)),
        "w_gate": dense(ks[15], (DEPTH, D_MODEL, D_FF), D_MODEL),
        "w_up": dense(ks[16], (DEPTH, D_MODEL, D_FF), D_MODEL),
        "w_down": dense(ks[17], (DEPTH, D_FF, D_MODEL), D_FF),
        "final_norm": gain(ks[18], (D_MODEL,)),
    }


def reference(x, positions, mix_norm, w_in, hg_lower_bounds, hg_out_norm, fox_f_bias, fox_out_norm,
              mla_q_a_norm, mla_w_q_b, mla_kv_a_norm, mla_w_kv_b, mla_out_norm, w_o,
              ffn_norm, w_gate, w_up, w_down, final_norm):
    inv_freq = ROPE_BASE ** (-jnp.arange(0, MLA_ROPE, 2, dtype=jnp.float32) / MLA_ROPE)
    angles = positions.astype(jnp.float32)[..., None] * inv_freq
    cos = jnp.cos(angles)[:, :, None, :]
    sin = jnp.sin(angles)[:, :, None, :]
    lb_soft = jax.nn.softmax(hg_lower_bounds.astype(jnp.float32), axis=0)
    lb_all = jnp.cumsum(lb_soft, axis=0) - lb_soft[0]

    for l in range(DEPTH):
        h = rms_norm(x, mix_norm[l])
        proj = h @ w_in[l]
        (hq, hf, hi, hg, fq, fk, fv, ff, mq, mkv) = split_cols(proj, IN_SPLIT)
        y_hg = hgrn2_mixer(hq, hf, hi, hg, lb_all[l], hg_out_norm[l])
        y_fox = rms_norm(fox_mixer(fq, fk, fv, ff, fox_f_bias[l]), fox_out_norm[l])
        y_mla = rms_norm(mla_mixer(mq, mkv, mla_q_a_norm[l], mla_w_q_b[l], mla_kv_a_norm[l],
                                   mla_w_kv_b[l], cos, sin), mla_out_norm[l])
        y = jnp.concatenate([y_hg.astype(x.dtype), y_fox.astype(x.dtype), y_mla.astype(x.dtype)], axis=-1)
        x = x + y @ w_o[l]
        h = rms_norm(x, ffn_norm[l])
        x = x + (jax.nn.silu(h @ w_gate[l]) * (h @ w_up[l])) @ w_down[l]
    return rms_norm(x, final_norm)
```

```python
import functools

import numpy as np
import jax
import jax.numpy as jnp
from jax import lax
from jax.experimental import pallas as pl
from jax.experimental.pallas import tpu as pltpu

F32 = jnp.float32
BF16 = jnp.bfloat16

LANES = 128
HEAD_DIM = 128
MLA_ROPE = 64
ROPE_BASE = 10000.0
EPS = 1e-6
MASK_VALUE = -1e30
HG_CHUNK = 128
HG_SUB = 16
VMEM_LIMIT_BYTES = 52 * 1024 * 1024


def _params(*semantics):
    return pltpu.CompilerParams(dimension_semantics=semantics,
                                vmem_limit_bytes=VMEM_LIMIT_BYTES)


def _pick(n, candidates):
    for c in candidates:
        if n % c == 0:
            return c
    return n


def _rmsnorm_kernel(x_ref, w_ref, o_ref):
    x = x_ref[...].astype(F32)
    ms = jnp.mean(x * x, axis=-1, keepdims=True)
    o_ref[...] = (x * lax.rsqrt(ms + EPS) * w_ref[...]).astype(o_ref.dtype)


def _rmsnorm(x, w, out_dtype):
    m, d = x.shape
    tm = _pick(m, (256, 128, 64, 32, 16, 8))
    return pl.pallas_call(
        _rmsnorm_kernel,
        grid=(m // tm,),
        in_specs=[pl.BlockSpec((tm, d), lambda i: (i, 0)),
                  pl.BlockSpec((1, d), lambda i: (0, 0))],
        out_specs=pl.BlockSpec((tm, d), lambda i: (i, 0)),
        out_shape=jax.ShapeDtypeStruct((m, d), out_dtype),
        compiler_params=_params("parallel"),
        name="rmsnorm",
    )(x, w.reshape(1, d).astype(F32))


def _mm_kernel(a_ref, w_ref, o_ref):
    o_ref[...] = jnp.dot(a_ref[...], w_ref[...],
                         preferred_element_type=F32).astype(o_ref.dtype)


def _mm_res_kernel(a_ref, w_ref, r_ref, o_ref):
    o_ref[...] = r_ref[...] + jnp.dot(a_ref[...], w_ref[...],
                                      preferred_element_type=F32)


def _mm_swiglu_kernel(a_ref, wg_ref, wu_ref, o_ref):
    a = a_ref[...]
    g = jnp.dot(a, wg_ref[...], preferred_element_type=F32)
    u = jnp.dot(a, wu_ref[...], preferred_element_type=F32)
    o_ref[...] = (g * (1.0 / (1.0 + jnp.exp(-g))) * u).astype(o_ref.dtype)


def _matmul(a, w, layer, out_dtype, tm, tn, residual=None, name="matmul"):
    m, k = a.shape
    n = w.shape[-1]
    tm = _pick(m, (tm, 512, 256, 128, 64, 32, 16, 8))
    tn = _pick(n, (tn, 512, 256, 128))
    in_specs = [pl.BlockSpec((tm, k), lambda i, j: (i, 0)),
                pl.BlockSpec((None, k, tn), lambda i, j: (layer, 0, j))]
    args = [a, w]
    kern = _mm_kernel
    if residual is not None:
        in_specs.append(pl.BlockSpec((tm, tn), lambda i, j: (i, j)))
        args.append(residual)
        kern = _mm_res_kernel
    return pl.pallas_call(
        kern,
        grid=(m // tm, n // tn),
        in_specs=in_specs,
        out_specs=pl.BlockSpec((tm, tn), lambda i, j: (i, j)),
        out_shape=jax.ShapeDtypeStruct((m, n), out_dtype),
        compiler_params=_params("parallel", "arbitrary"),
        name=name,
    )(*args)


def _swiglu_up(a, wg, wu, layer, tm, tn):
    m, k = a.shape
    n = wg.shape[-1]
    tm = _pick(m, (tm, 512, 256, 128, 64, 32, 16, 8))
    tn = _pick(n, (tn, 256, 128))
    w_spec = pl.BlockSpec((None, k, tn), lambda i, j: (layer, 0, j))
    return pl.pallas_call(
        _mm_swiglu_kernel,
        grid=(m // tm, n // tn),
        in_specs=[pl.BlockSpec((tm, k), lambda i, j: (i, 0)), w_spec, w_spec],
        out_specs=pl.BlockSpec((tm, tn), lambda i, j: (i, j)),
        out_shape=jax.ShapeDtypeStruct((m, n), BF16),
        compiler_params=_params("parallel", "arbitrary"),
        name="swiglu_up",
    )(a, wg, wu)


def _split3(x):
    hi = x.astype(BF16)
    r1 = x - hi.astype(F32)
    mid = r1.astype(BF16)
    lo = (r1 - mid.astype(F32)).astype(BF16)
    return hi, mid, lo


def _tri_cat(n):
    tri = np.tril(np.ones((n, n), np.float32))
    return jnp.asarray(np.concatenate([tri, tri, tri], axis=1), dtype=BF16)


def _cumsum_rows(x, lcat):
    hi, mid, lo = _split3(x)
    stacked = jnp.concatenate([hi, mid, lo], axis=0)
    return jnp.dot(lcat, stacked, preferred_element_type=F32)


def _hgrn2_diag(qi, ki, bi, base, colid, rowid):
    a = jnp.zeros(colid.shape, F32)
    for s in range(HG_SUB):
        d = jnp.exp(jnp.minimum(bi - bi[s:s + 1, :], 0.0))
        col = jnp.sum(d * qi * ki[s:s + 1, :], axis=-1, keepdims=True)
        a = jnp.where(colid == base + s, col, a)
    return jnp.where(colid <= rowid + base, a, 0.0)


def _hgrn2_kernel(q_ref, f_ref, i_ref, lb_ref, lcat_ref, o_ref, st_ref, *,
                  nchunks, k_scale):
    @pl.when(pl.program_id(2) == 0)
    def _():
        st_ref[...] = jnp.zeros_like(st_ref)

    lb = lb_ref[...]
    one_m_lb = 1.0 - lb
    nsub = HG_CHUNK // HG_SUB
    colid = lax.broadcasted_iota(jnp.int32, (HG_SUB, HG_CHUNK), 1)
    rowid = lax.broadcasted_iota(jnp.int32, (HG_SUB, HG_CHUNK), 0)

    def chunk(c, carry):
        r0 = pl.multiple_of(c * HG_CHUNK, HG_CHUNK)
        qp = q_ref[pl.ds(r0, HG_CHUNK), :].astype(F32)
        z = f_ref[pl.ds(r0, HG_CHUNK), :].astype(F32)
        v = i_ref[pl.ds(r0, HG_CHUNK), :]
        q = qp * (1.0 / (1.0 + jnp.exp(-qp))) * k_scale
        t = jnp.exp(-jnp.abs(z))
        inv = 1.0 / (1.0 + t)
        pos = z >= 0.0
        sig = jnp.where(pos, inv, t * inv)
        nsig = jnp.where(pos, t * inv, inv)
        kk = one_m_lb * nsig
        lf = jnp.log(lb + one_m_lb * sig)
        b = _cumsum_rows(lf, lcat_ref[...])
        b_last = b[HG_CHUNK - 1:HG_CHUNK, :]

        st = st_ref[...]
        qd = (q * jnp.exp(b)).astype(BF16)
        o = lax.dot_general(qd, st.astype(BF16), (((1,), (1,)), ((), ())),
                            preferred_element_type=F32)

        rows = []
        for i in range(nsub):
            lo_r = i * HG_SUB
            bi = b[lo_r:lo_r + HG_SUB, :]
            qi = q[lo_r:lo_r + HG_SUB, :]
            ki = kk[lo_r:lo_r + HG_SUB, :]
            a_i = _hgrn2_diag(qi, ki, bi, lo_r, colid, rowid)
            if i > 0:
                r = b[lo_r - 1:lo_r, :]
                qr = (qi * jnp.exp(bi - r)).astype(BF16)
                kr = (kk * jnp.exp(jnp.minimum(r - b, 0.0))).astype(BF16)
                a_off = lax.dot_general(qr, kr, (((1,), (1,)), ((), ())),
                                        preferred_element_type=F32)
                a_i = jnp.where(colid < lo_r, a_off, a_i)
            rows.append(a_i)
        a = jnp.concatenate(rows, axis=0).astype(BF16)
        o = o + jnp.dot(a, v, preferred_element_type=F32)
        o_ref[pl.ds(r0, HG_CHUNK), :] = o.astype(o_ref.dtype)

        kd = (kk * jnp.exp(b_last - b)).astype(BF16)
        upd = lax.dot_general(v, kd, (((0,), (0,)), ((), ())),
                              preferred_element_type=F32)
        st_ref[...] = st * jnp.exp(b_last) + upd
        return carry

    lax.fori_loop(0, nchunks, chunk, 0)


def _hgrn2(proj, lb_all, layer, batch, seq, heads, lcat):
    m = proj.shape[0]
    t = _pick(seq, (512, 256, 128))
    nt = seq // t

    def col(off):
        return pl.BlockSpec((t, HEAD_DIM), lambda b, h, s: (b * nt + s, off + h))

    kern = functools.partial(_hgrn2_kernel, nchunks=t // HG_CHUNK,
                             k_scale=float(HEAD_DIM) ** -0.5)
    return pl.pallas_call(
        kern,
        grid=(batch, heads, nt),
        in_specs=[col(0), col(heads), col(2 * heads),
                  pl.BlockSpec((None, 1, HEAD_DIM), lambda b, h, s: (layer, 0, h)),
                  pl.BlockSpec(lcat.shape, lambda b, h, s: (0, 0))],
        out_specs=pl.BlockSpec((t, HEAD_DIM), lambda b, h, s: (b * nt + s, h)),
        out_shape=jax.ShapeDtypeStruct((m, heads * HEAD_DIM), BF16),
        scratch_shapes=[pltpu.VMEM((HEAD_DIM, HEAD_DIM), F32)],
        compiler_params=_params("parallel", "parallel", "arbitrary"),
        name="hgrn2",
    )(proj, proj, proj, lb_all, lcat)


def _fox_prep_kernel(ff_ref, bias_ref, lcat_ref, pq_ref, pk_ref, cq_ref, ck_ref,
                     qe_ref, ke_ref, carry_ref):
    @pl.when(pl.program_id(1) == 0)
    def _():
        carry_ref[...] = jnp.zeros_like(carry_ref)

    x = ff_ref[...] + bias_ref[...]
    lf = jnp.minimum(x, 0.0) - jnp.log(1.0 + jnp.exp(-jnp.abs(x)))
    c = _cumsum_rows(lf, lcat_ref[...]) + carry_ref[...]
    carry_ref[...] = c[c.shape[0] - 1:, :]
    e = jnp.concatenate(_split3(c), axis=1)
    qe = jnp.dot(e, pq_ref[...], preferred_element_type=F32) + cq_ref[...]
    ke = jnp.dot(e, pk_ref[...], preferred_element_type=F32) + ck_ref[...]
    qe_ref[...] = qe.astype(qe_ref.dtype)
    ke_ref[...] = ke.astype(ke_ref.dtype)


def _fox_expanders(heads):
    width = heads * HEAD_DIM
    pq = np.zeros((3 * LANES, width), np.float32)
    pk = np.zeros((3 * LANES, width), np.float32)
    cq = np.zeros((1, width), np.float32)
    ck = np.zeros((1, width), np.float32)
    for h in range(heads):
        for part in range(3):
            pq[part * LANES + h, h * HEAD_DIM + part] = 1.0
            pk[part * LANES + h, h * HEAD_DIM + 3 + part] = -1.0
            cq[0, h * HEAD_DIM + 3 + part] = 1.0
            ck[0, h * HEAD_DIM + part] = 1.0
    return (jnp.asarray(pq, BF16), jnp.asarray(pk, BF16),
            jnp.asarray(cq, F32), jnp.asarray(ck, F32))


def _fox_prep(ff, bias_row, batch, seq, heads):
    m = ff.shape[0]
    blk = _pick(seq, (256, 128))
    nb = seq // blk
    width = heads * HEAD_DIM
    lcat = _tri_cat(blk)
    pq, pk, cq, ck = _fox_expanders(heads)
    full = lambda a: pl.BlockSpec(a.shape, lambda b, s: (0, 0))
    out_spec = pl.BlockSpec((blk, width), lambda b, s: (b * nb + s, 0))
    return pl.pallas_call(
        _fox_prep_kernel,
        grid=(batch, nb),
        in_specs=[pl.BlockSpec((blk, LANES), lambda b, s: (b * nb + s, 0)),
                  full(bias_row), full(lcat), full(pq), full(pk), full(cq), full(ck)],
        out_specs=[out_spec, out_spec],
        out_shape=[jax.ShapeDtypeStruct((m, width), BF16)] * 2,
        scratch_shapes=[pltpu.VMEM((1, LANES), F32)],
        compiler_params=_params("parallel", "arbitrary"),
        name="fox_prep",
    )(ff, bias_row, lcat, pq, pk, cq, ck)


def _rope(t, cos_t, sin_t):
    half = MLA_ROPE // 2
    lane = lax.broadcasted_iota(jnp.int32, t.shape, 1)
    swapped = jnp.where(lane < half,
                        pltpu.roll(t, LANES - half, 1),
                        pltpu.roll(t, half, 1))
    return t * cos_t + swapped * sin_t


def _flash_kernel(qi_ref, kj_ref, *refs, rope_q):
    if rope_q:
        (q1_ref, q2_ref, cos_ref, sin_ref, k1_ref, k2_ref, v_ref,
         o_ref, qf_ref, m_ref, l_ref, acc_ref) = refs
    else:
        (q1_ref, q2_ref, k1_ref, k2_ref, v_ref,
         o_ref, qf_ref, m_ref, l_ref, acc_ref) = refs
    p_idx = pl.program_id(2)
    qi = qi_ref[p_idx]
    kj = kj_ref[p_idx]

    @pl.when(kj == 0)
    def _():
        qf_ref[:, :HEAD_DIM] = q1_ref[...]
        if rope_q:
            q2 = _rope(q2_ref[...].astype(F32), cos_ref[...], sin_ref[...])
            qf_ref[:, HEAD_DIM:] = q2.astype(qf_ref.dtype)
        else:
            qf_ref[:, HEAD_DIM:] = q2_ref[...]
        m_ref[...] = jnp.full_like(m_ref, MASK_VALUE)
        l_ref[...] = jnp.zeros_like(l_ref)
        acc_ref[...] = jnp.zeros_like(acc_ref)

    def step(masked):
        kf = jnp.concatenate([k1_ref[...], k2_ref[...]], axis=1)
        s = lax.dot_general(qf_ref[...], kf, (((1,), (1,)), ((), ())),
                            preferred_element_type=F32)
        if masked:
            row = lax.broadcasted_iota(jnp.int32, s.shape, 0)
            col = lax.broadcasted_iota(jnp.int32, s.shape, 1)
            s = jnp.where(col <= row, s, MASK_VALUE)
        m_prev = m_ref[...]
        m_new = jnp.maximum(m_prev, jnp.max(s, axis=-1, keepdims=True))
        alpha = jnp.exp(m_prev - m_new)
        p = jnp.exp(s - m_new)
        l_ref[...] = alpha * l_ref[...] + jnp.sum(p, axis=-1, keepdims=True)
        acc_ref[...] = alpha * acc_ref[...] + jnp.dot(
            p.astype(v_ref.dtype), v_ref[...], preferred_element_type=F32)
        m_ref[...] = m_new

    @pl.when(kj < qi)
    def _():
        step(False)

    @pl.when(kj == qi)
    def _():
        step(True)
        o_ref[...] = (acc_ref[...] / l_ref[...]).astype(o_ref.dtype)


def _flash(q1, q1_off, q2, q2_off, k1, k1_off, k2, k2_off, k2_per_head,
           v, v_off, batch, seq, heads, rope=None):
    m = q1.shape[0]
    t = _pick(seq, (512, 256, 128))
    nq = seq // t
    pairs = [(i, j) for i in range(nq) for j in range(i + 1)]
    qi_tab = jnp.asarray([p[0] for p in pairs], jnp.int32)
    kj_tab = jnp.asarray([p[1] for p in pairs], jnp.int32)

    def qspec(off):
        return pl.BlockSpec((t, HEAD_DIM),
                            lambda b, h, p, qi, kj: (b * nq + qi[p], off + h))

    def kspec(off, per_head=True):
        if per_head:
            return pl.BlockSpec((t, HEAD_DIM),
                                lambda b, h, p, qi, kj: (b * nq + kj[p], off + h))
        return pl.BlockSpec((t, HEAD_DIM),
                            lambda b, h, p, qi, kj: (b * nq + kj[p], off))

    in_specs = [qspec(q1_off), qspec(q2_off)]
    args = [q1, q2]
    if rope is not None:
        tab = pl.BlockSpec((t, LANES), lambda b, h, p, qi, kj: (b * nq + qi[p], 0))
        in_specs += [tab, tab]
        args += list(rope)
    in_specs += [kspec(k1_off), kspec(k2_off, k2_per_head), kspec(v_off)]
    args += [k1, k2, v]
    return pl.pallas_call(
        functools.partial(_flash_kernel, rope_q=rope is not None),
        grid_spec=pltpu.PrefetchScalarGridSpec(
            num_scalar_prefetch=2,
            grid=(batch, heads, len(pairs)),
            in_specs=in_specs,
            out_specs=pl.BlockSpec((t, HEAD_DIM),
                                   lambda b, h, p, qi, kj: (b * nq + qi[p], h)),
            scratch_shapes=[pltpu.VMEM((t, 2 * HEAD_DIM), BF16),
                            pltpu.VMEM((t, 1), F32),
                            pltpu.VMEM((t, 1), F32),
                            pltpu.VMEM((t, HEAD_DIM), F32)]),
        out_shape=jax.ShapeDtypeStruct((m, heads * HEAD_DIM), BF16),
        compiler_params=_params("parallel", "parallel", "arbitrary"),
        name="flash_rope" if rope is not None else "flash",
    )(qi_tab, kj_tab, *args)


def _mla_prep_kernel(mq_ref, mkv_ref, qw_ref, kvw_ref, cos_ref, sin_ref,
                     qn_ref, kvn_ref, kpe_ref, *, kv_rank):
    mq = mq_ref[...].astype(F32)
    ms = jnp.mean(mq * mq, axis=-1, keepdims=True)
    qn_ref[...] = (mq * lax.rsqrt(ms + EPS) * qw_ref[...]).astype(qn_ref.dtype)
    lat = mkv_ref[:, :kv_rank].astype(F32)
    ms = jnp.mean(lat * lat, axis=-1, keepdims=True)
    kvn_ref[...] = (lat * lax.rsqrt(ms + EPS) * kvw_ref[...]).astype(kvn_ref.dtype)
    kpe = mkv_ref[:, kv_rank:kv_rank + LANES].astype(F32)
    kpe_ref[...] = _rope(kpe, cos_ref[...], sin_ref[...]).astype(kpe_ref.dtype)


def _mla_prep(proj, mq_blk, mkv_blk, q_rank, kv_rank, qw, kvw, cos_t, sin_t):
    m = proj.shape[0]
    tm = _pick(m, (256, 128, 64, 32, 16, 8))
    row = lambda width, blk: pl.BlockSpec((tm, width), lambda i: (i, blk))
    full = lambda a: pl.BlockSpec(a.shape, lambda i: (0, 0))
    return pl.pallas_call(
        functools.partial(_mla_prep_kernel, kv_rank=kv_rank),
        grid=(m // tm,),
        in_specs=[row(q_rank, mq_blk), row(q_rank, mkv_blk), full(qw), full(kvw),
                  row(LANES, 0), row(LANES, 0)],
        out_specs=[row(q_rank, 0), row(kv_rank, 0), row(LANES, 0)],
        out_shape=[jax.ShapeDtypeStruct((m, q_rank), BF16),
                   jax.ShapeDtypeStruct((m, kv_rank), BF16),
                   jax.ShapeDtypeStruct((m, LANES), BF16)],
        compiler_params=_params("parallel"),
        name="mla_prep",
    )(proj, proj, qw, kvw, cos_t, sin_t)


def _postmix_kernel(ohg_ref, g_ref, ofox_ref, omla_ref, hgw_ref, fw_ref, mw_ref,
                    y_ref, *, hg_w, fox_w):
    def normed(o_ref, w_ref):
        o = o_ref[...].astype(F32)
        ms = jnp.mean(o * o, axis=-1, keepdims=True)
        return o * lax.rsqrt(ms + EPS) * w_ref[...]

    g = g_ref[...].astype(F32)
    y_hg = normed(ohg_ref, hgw_ref) * (g * (1.0 / (1.0 + jnp.exp(-g))))
    y_ref[:, :hg_w] = y_hg.astype(y_ref.dtype)
    y_ref[:, hg_w:hg_w + fox_w] = normed(ofox_ref, fw_ref).astype(y_ref.dtype)
    y_ref[:, hg_w + fox_w:] = normed(omla_ref, mw_ref).astype(y_ref.dtype)


def _postmix(o_hg, proj, g_blk, o_fox, o_mla, hgw, fw, mw):
    m, hg_w = o_hg.shape
    fox_w = o_fox.shape[1]
    mla_w = o_mla.shape[1]
    tm = _pick(m, (256, 128, 64, 32, 16, 8))
    row = lambda width, blk: pl.BlockSpec((tm, width), lambda i: (i, blk))
    full = lambda a: pl.BlockSpec(a.shape, lambda i: (0, 0))
    return pl.pallas_call(
        functools.partial(_postmix_kernel, hg_w=hg_w, fox_w=fox_w),
        grid=(m // tm,),
        in_specs=[row(hg_w, 0), row(hg_w, g_blk), row(fox_w, 0), row(mla_w, 0),
                  full(hgw), full(fw), full(mw)],
        out_specs=row(hg_w + fox_w + mla_w, 0),
        out_shape=jax.ShapeDtypeStruct((m, hg_w + fox_w + mla_w), BF16),
        compiler_params=_params("parallel"),
        name="postmix",
    )(o_hg, proj, o_fox, o_mla, hgw, fw, mw)


def kernel(x, positions, mix_norm, w_in, hg_lower_bounds, hg_out_norm, fox_f_bias,
           fox_out_norm, mla_q_a_norm, mla_w_q_b, mla_kv_a_norm, mla_w_kv_b,
           mla_out_norm, w_o, ffn_norm, w_gate, w_up, w_down, final_norm):
    batch, seq, d_model = x.shape
    depth = w_in.shape[0]
    m = batch * seq
    hg_w = hg_lower_bounds.shape[1]
    hg_heads = hg_w // HEAD_DIM
    fox_heads = fox_f_bias.shape[1]
    fox_w = fox_heads * HEAD_DIM
    q_rank = mla_q_a_norm.shape[1]
    kv_rank = mla_kv_a_norm.shape[1]
    mla_heads = mla_w_q_b.shape[2] // (HEAD_DIM + MLA_ROPE)
    mla_w = mla_heads * HEAD_DIM
    assert kv_rank + LANES <= q_rank and fox_heads <= LANES

    off_fox = 4 * hg_w
    off_ff = off_fox + 3 * fox_w
    off_mq = off_ff + fox_heads
    off_mkv = off_mq + q_rank
    fox_scale = float(HEAD_DIM) ** -0.5
    mla_scale = float(HEAD_DIM + MLA_ROPE) ** -0.5
    pad_kv = q_rank - (kv_rank + MLA_ROPE)
    w_proj = jnp.concatenate([
        w_in[:, :, :off_fox],
        w_in[:, :, off_mq:off_mkv],
        w_in[:, :, off_mkv:],
        jnp.zeros((depth, d_model, pad_kv), w_in.dtype),
        w_in[:, :, off_fox:off_fox + fox_w] * fox_scale,
        w_in[:, :, off_fox + fox_w:off_ff],
    ], axis=2).astype(BF16)
    w_ff = jnp.pad(w_in[:, :, off_ff:off_mq],
                   ((0, 0), (0, 0), (0, LANES - fox_heads))).astype(BF16)
    blk_mq = (4 * hg_w) // q_rank
    blk_mkv = blk_mq + 1
    blk_fq = (4 * hg_w + 2 * q_rank) // HEAD_DIM
    blk_fk = blk_fq + fox_heads
    blk_fv = blk_fk + fox_heads
    assert (4 * hg_w) % q_rank == 0 and hg_w == q_rank

    wq = mla_w_q_b.reshape(depth, q_rank, mla_heads, HEAD_DIM + MLA_ROPE) * mla_scale
    wq_rope = jnp.pad(wq[..., HEAD_DIM:], ((0, 0),) * 3 + ((0, HEAD_DIM - MLA_ROPE),))
    w_qb = jnp.concatenate([wq[..., :HEAD_DIM].reshape(depth, q_rank, mla_w),
                            wq_rope.reshape(depth, q_rank, mla_w)], axis=2).astype(BF16)
    wkv = mla_w_kv_b.reshape(depth, kv_rank, mla_heads, 2 * HEAD_DIM)
    w_kvb = jnp.concatenate([wkv[..., :HEAD_DIM].reshape(depth, kv_rank, mla_w),
                             wkv[..., HEAD_DIM:].reshape(depth, kv_rank, mla_w)],
                            axis=2).astype(BF16)
    w_o_b = w_o.astype(BF16)
    w_gate_b = w_gate.astype(BF16)
    w_up_b = w_up.astype(BF16)
    w_down_b = w_down.astype(BF16)

    lb_soft = jax.nn.softmax(hg_lower_bounds.astype(F32), axis=0)
    lb_all = (jnp.cumsum(lb_soft, axis=0) - lb_soft[0]).reshape(depth, 1, hg_w)
    fox_bias = jnp.pad(fox_f_bias.astype(F32),
                       ((0, 0), (0, LANES - fox_heads))).reshape(depth, 1, LANES)

    inv_freq = ROPE_BASE ** (-jnp.arange(0, MLA_ROPE, 2, dtype=F32) / MLA_ROPE)
    angles = positions.astype(F32).reshape(m, 1) * inv_freq
    cos, sin = jnp.cos(angles), jnp.sin(angles)
    zpad = jnp.zeros((m, LANES - MLA_ROPE), F32)
    cos_t = jnp.concatenate([cos, cos, zpad], axis=1)
    sin_t = jnp.concatenate([-sin, sin, zpad], axis=1)

    hg_lcat = _tri_cat(HG_CHUNK)
    xr = x.reshape(m, d_model).astype(F32)

    for l in range(depth):
        h = _rmsnorm(xr, mix_norm[l], BF16)
        proj = _matmul(h, w_proj, l, BF16, 1024, 512, name="in_proj")
        ff = _matmul(h, w_ff, l, F32, 1024, LANES, name="ff_proj")

        o_hg = _hgrn2(proj, lb_all, l, batch, seq, hg_heads, hg_lcat)

        qe, ke = _fox_prep(ff, fox_bias[l], batch, seq, fox_heads)
        o_fox = _flash(proj, blk_fq, qe, 0, proj, blk_fk, ke, 0, True,
                       proj, blk_fv, batch, seq, fox_heads)

        qn, kvn, kpe = _mla_prep(proj, blk_mq, blk_mkv, q_rank, kv_rank,
                                 mla_q_a_norm[l].reshape(1, q_rank).astype(F32),
                                 mla_kv_a_norm[l].reshape(1, kv_rank).astype(F32),
                                 cos_t, sin_t)
        q_mla = _matmul(qn, w_qb, l, BF16, 1024, 512, name="mla_q_b")
        kv_mla = _matmul(kvn, w_kvb, l, BF16, 1024, 512, name="mla_kv_b")
        o_mla = _flash(q_mla, 0, q_mla, mla_heads, kv_mla, 0, kpe, 0, False,
                       kv_mla, mla_heads, batch, seq, mla_heads,
                       rope=(cos_t, sin_t))

        y = _postmix(o_hg, proj, 3, o_fox, o_mla,
                     hg_out_norm[l].reshape(1, hg_w).astype(F32),
                     fox_out_norm[l].reshape(1, fox_w).astype(F32),
                     mla_out_norm[l].reshape(1, mla_w).astype(F32))
        xr = _matmul(y, w_o_b, l, F32, 1024, 512, residual=xr, name="out_proj")

        h = _rmsnorm(xr, ffn_norm[l], BF16)
        act = _swiglu_up(h, w_gate_b, w_up_b, l, 1024, 256)
        xr = _matmul(act, w_down_b, l, F32, 512, 256, residual=xr, name="down_proj")

    out = _rmsnorm(xr, final_norm, x.dtype)
    return out.reshape(batch, seq, d_model)
```

```python
import functools
import math

import numpy as np
import jax
import jax.numpy as jnp
from jax import lax
from jax.experimental import pallas as pl
from jax.experimental.pallas import tpu as pltpu

F32 = jnp.float32
BF16 = jnp.bfloat16

LANES = 128
HEAD_DIM = 128
MLA_ROPE = 64
ROPE_BASE = 10000.0
EPS = 1e-6
MASK_VALUE = -1e30
LOG2E = math.log2(math.e)
HG_CHUNK = 128
HG_SUB = 16
FLASH_TQ = 2048
FLASH_TK = 2048
FLASH_CHAINS = 4
VMEM_LIMIT_BYTES = 52 * 1024 * 1024


def _params(*semantics):
    return pltpu.CompilerParams(dimension_semantics=semantics,
                                vmem_limit_bytes=VMEM_LIMIT_BYTES)


def _pick(n, candidates):
    for c in candidates:
        if n % c == 0:
            return c
    return n


def _rmsnorm_kernel(x_ref, w_ref, o_ref):
    x = x_ref[...].astype(F32)
    ms = jnp.mean(x * x, axis=-1, keepdims=True)
    o_ref[...] = (x * lax.rsqrt(ms + EPS) * w_ref[...]).astype(o_ref.dtype)


def _rmsnorm(x, w, out_dtype):
    m, d = x.shape
    tm = _pick(m, (256, 128, 64, 32, 16, 8))
    return pl.pallas_call(
        _rmsnorm_kernel,
        grid=(m // tm,),
        in_specs=[pl.BlockSpec((tm, d), lambda i: (i, 0)),
                  pl.BlockSpec((1, d), lambda i: (0, 0))],
        out_specs=pl.BlockSpec((tm, d), lambda i: (i, 0)),
        out_shape=jax.ShapeDtypeStruct((m, d), out_dtype),
        compiler_params=_params("parallel"),
        name="rmsnorm",
    )(x, w.reshape(1, d).astype(F32))


def _mm_kernel(a_ref, w_ref, o_ref):
    o_ref[...] = jnp.dot(a_ref[...], w_ref[...],
                         preferred_element_type=F32).astype(o_ref.dtype)


def _mm_res_kernel(a_ref, w_ref, r_ref, o_ref):
    o_ref[...] = r_ref[...] + jnp.dot(a_ref[...], w_ref[...],
                                      preferred_element_type=F32)


def _mm_swiglu_kernel(a_ref, wg_ref, wu_ref, o_ref):
    a = a_ref[...]
    g = jnp.dot(a, wg_ref[...], preferred_element_type=F32)
    u = jnp.dot(a, wu_ref[...], preferred_element_type=F32)
    o_ref[...] = (g * (1.0 / (1.0 + jnp.exp(-g))) * u).astype(o_ref.dtype)


def _matmul(a, w, layer, out_dtype, tm, tn, residual=None, name="matmul"):
    m, k = a.shape
    n = w.shape[-1]
    tm = _pick(m, (tm, 512, 256, 128, 64, 32, 16, 8))
    tn = _pick(n, (tn, 512, 256, 128))
    in_specs = [pl.BlockSpec((tm, k), lambda i, j: (i, 0)),
                pl.BlockSpec((None, k, tn), lambda i, j: (layer, 0, j))]
    args = [a, w]
    kern = _mm_kernel
    if residual is not None:
        in_specs.append(pl.BlockSpec((tm, tn), lambda i, j: (i, j)))
        args.append(residual)
        kern = _mm_res_kernel
    return pl.pallas_call(
        kern,
        grid=(m // tm, n // tn),
        in_specs=in_specs,
        out_specs=pl.BlockSpec((tm, tn), lambda i, j: (i, j)),
        out_shape=jax.ShapeDtypeStruct((m, n), out_dtype),
        compiler_params=_params("parallel", "arbitrary"),
        name=name,
    )(*args)


def _swiglu_up(a, wg, wu, layer, tm, tn):
    m, k = a.shape
    n = wg.shape[-1]
    tm = _pick(m, (tm, 512, 256, 128, 64, 32, 16, 8))
    tn = _pick(n, (tn, 256, 128))
    w_spec = pl.BlockSpec((None, k, tn), lambda i, j: (layer, 0, j))
    return pl.pallas_call(
        _mm_swiglu_kernel,
        grid=(m // tm, n // tn),
        in_specs=[pl.BlockSpec((tm, k), lambda i, j: (i, 0)), w_spec, w_spec],
        out_specs=pl.BlockSpec((tm, tn), lambda i, j: (i, j)),
        out_shape=jax.ShapeDtypeStruct((m, n), BF16),
        compiler_params=_params("parallel", "arbitrary"),
        name="swiglu_up",
    )(a, wg, wu)


def _split3(x):
    hi = x.astype(BF16)
    r1 = x - hi.astype(F32)
    mid = r1.astype(BF16)
    lo = (r1 - mid.astype(F32)).astype(BF16)
    return hi, mid, lo


def _tri_cat(n):
    tri = np.tril(np.ones((n, n), np.float32))
    return jnp.asarray(np.concatenate([tri, tri, tri], axis=1), dtype=BF16)


def _cumsum_rows(x, lcat):
    hi, mid, lo = _split3(x)
    stacked = jnp.concatenate([hi, mid, lo], axis=0)
    return jnp.dot(lcat, stacked, preferred_element_type=F32)


def _hgrn2_diag(qi, ki, bi, base, colid, rowid):
    a = jnp.zeros(colid.shape, F32)
    for s in range(HG_SUB):
        d = jnp.exp(jnp.minimum(bi - bi[s:s + 1, :], 0.0))
        col = jnp.sum(d * qi * ki[s:s + 1, :], axis=-1, keepdims=True)
        a = jnp.where(colid == base + s, col, a)
    return jnp.where(colid <= rowid + base, a, 0.0)


def _hgrn2_kernel(q_ref, f_ref, i_ref, lb_ref, lcat_ref, o_ref, st_ref, *,
                  nchunks, k_scale):
    @pl.when(pl.program_id(2) == 0)
    def _():
        st_ref[...] = jnp.zeros_like(st_ref)

    lb = lb_ref[...]
    one_m_lb = 1.0 - lb
    nsub = HG_CHUNK // HG_SUB
    colid = lax.broadcasted_iota(jnp.int32, (HG_SUB, HG_CHUNK), 1)
    rowid = lax.broadcasted_iota(jnp.int32, (HG_SUB, HG_CHUNK), 0)

    def chunk(c, carry):
        r0 = pl.multiple_of(c * HG_CHUNK, HG_CHUNK)
        qp = q_ref[pl.ds(r0, HG_CHUNK), :].astype(F32)
        z = f_ref[pl.ds(r0, HG_CHUNK), :].astype(F32)
        v = i_ref[pl.ds(r0, HG_CHUNK), :]
        q = qp * (1.0 / (1.0 + jnp.exp(-qp))) * k_scale
        t = jnp.exp(-jnp.abs(z))
        inv = 1.0 / (1.0 + t)
        pos = z >= 0.0
        sig = jnp.where(pos, inv, t * inv)
        nsig = jnp.where(pos, t * inv, inv)
        kk = one_m_lb * nsig
        lf = jnp.log(lb + one_m_lb * sig)
        b = _cumsum_rows(lf, lcat_ref[...])
        b_last = b[HG_CHUNK - 1:HG_CHUNK, :]

        st = st_ref[...]
        qd = (q * jnp.exp(b)).astype(BF16)
        o = lax.dot_general(qd, st.astype(BF16), (((1,), (1,)), ((), ())),
                            preferred_element_type=F32)

        rows = []
        for i in range(nsub):
            lo_r = i * HG_SUB
            bi = b[lo_r:lo_r + HG_SUB, :]
            qi = q[lo_r:lo_r + HG_SUB, :]
            ki = kk[lo_r:lo_r + HG_SUB, :]
            a_i = _hgrn2_diag(qi, ki, bi, lo_r, colid, rowid)
            if i > 0:
                r = b[lo_r - 1:lo_r, :]
                qr = (qi * jnp.exp(bi - r)).astype(BF16)
                kr = (kk * jnp.exp(jnp.minimum(r - b, 0.0))).astype(BF16)
                a_off = lax.dot_general(qr, kr, (((1,), (1,)), ((), ())),
                                        preferred_element_type=F32)
                a_i = jnp.where(colid < lo_r, a_off, a_i)
            rows.append(a_i)
        a = jnp.concatenate(rows, axis=0).astype(BF16)
        o = o + jnp.dot(a, v, preferred_element_type=F32)
        o_ref[pl.ds(r0, HG_CHUNK), :] = o.astype(o_ref.dtype)

        kd = (kk * jnp.exp(b_last - b)).astype(BF16)
        upd = lax.dot_general(v, kd, (((0,), (0,)), ((), ())),
                              preferred_element_type=F32)
        st_ref[...] = st * jnp.exp(b_last) + upd
        return carry

    lax.fori_loop(0, nchunks, chunk, 0)


def _hgrn2(proj, lb_all, layer, batch, seq, heads, lcat):
    m = proj.shape[0]
    t = _pick(seq, (512, 256, 128))
    nt = seq // t

    def col(off):
        return pl.BlockSpec((t, HEAD_DIM), lambda b, h, s: (b * nt + s, off + h))

    kern = functools.partial(_hgrn2_kernel, nchunks=t // HG_CHUNK,
                             k_scale=float(HEAD_DIM) ** -0.5)
    return pl.pallas_call(
        kern,
        grid=(batch, heads, nt),
        in_specs=[col(0), col(heads), col(2 * heads),
                  pl.BlockSpec((None, 1, HEAD_DIM), lambda b, h, s: (layer, 0, h)),
                  pl.BlockSpec(lcat.shape, lambda b, h, s: (0, 0))],
        out_specs=pl.BlockSpec((t, HEAD_DIM), lambda b, h, s: (b * nt + s, h)),
        out_shape=jax.ShapeDtypeStruct((m, heads * HEAD_DIM), BF16),
        scratch_shapes=[pltpu.VMEM((HEAD_DIM, HEAD_DIM), F32)],
        compiler_params=_params("parallel", "parallel", "arbitrary"),
        name="hgrn2",
    )(proj, proj, proj, lb_all, lcat)


def _fox_prep_kernel(ff_ref, bias_ref, fq_ref, fk_ref, lcat_ref, pq_ref, pk_ref,
                     cq_ref, ck_ref, qcat_ref, kcat_ref, carry_ref, *, heads):
    @pl.when(pl.program_id(1) == 0)
    def _():
        carry_ref[...] = jnp.zeros_like(carry_ref)

    x = ff_ref[...] + bias_ref[...]
    lf = jnp.minimum(x, 0.0) - jnp.log(1.0 + jnp.exp(-jnp.abs(x)))
    c = _cumsum_rows(lf, lcat_ref[...]) + carry_ref[...]
    carry_ref[...] = c[c.shape[0] - 1:, :]
    e = jnp.concatenate(_split3(c * LOG2E), axis=1)
    qe = jnp.dot(e, pq_ref[...], preferred_element_type=F32) + cq_ref[...]
    ke = jnp.dot(e, pk_ref[...], preferred_element_type=F32) + ck_ref[...]
    for h in range(heads):
        src = slice(h * HEAD_DIM, (h + 1) * HEAD_DIM)
        lo = 2 * h * HEAD_DIM
        qcat_ref[:, lo:lo + HEAD_DIM] = fq_ref[:, src]
        qcat_ref[:, lo + HEAD_DIM:lo + 2 * HEAD_DIM] = qe[:, src].astype(qcat_ref.dtype)
        kcat_ref[:, lo:lo + HEAD_DIM] = fk_ref[:, src]
        kcat_ref[:, lo + HEAD_DIM:lo + 2 * HEAD_DIM] = ke[:, src].astype(kcat_ref.dtype)


def _fox_expanders(heads):
    width = heads * HEAD_DIM
    pq = np.zeros((3 * LANES, width), np.float32)
    pk = np.zeros((3 * LANES, width), np.float32)
    cq = np.zeros((1, width), np.float32)
    ck = np.zeros((1, width), np.float32)
    for h in range(heads):
        for part in range(3):
            pq[part * LANES + h, h * HEAD_DIM + part] = 1.0
            pk[part * LANES + h, h * HEAD_DIM + 3 + part] = -1.0
            cq[0, h * HEAD_DIM + 3 + part] = 1.0
            ck[0, h * HEAD_DIM + part] = 1.0
    return (jnp.asarray(pq, BF16), jnp.asarray(pk, BF16),
            jnp.asarray(cq, F32), jnp.asarray(ck, F32))


def _fox_prep(ff, bias_row, proj, fq_slab, fk_slab, batch, seq, heads):
    m = ff.shape[0]
    blk = _pick(seq, (256, 128))
    nb = seq // blk
    width = heads * HEAD_DIM
    lcat = _tri_cat(blk)
    pq, pk, cq, ck = _fox_expanders(heads)
    full = lambda a: pl.BlockSpec(a.shape, lambda b, s: (0, 0))
    slab = lambda cb: pl.BlockSpec((blk, width), lambda b, s: (b * nb + s, cb))
    out_spec = pl.BlockSpec((blk, 2 * width), lambda b, s: (b * nb + s, 0))
    return pl.pallas_call(
        functools.partial(_fox_prep_kernel, heads=heads),
        grid=(batch, nb),
        in_specs=[pl.BlockSpec((blk, LANES), lambda b, s: (b * nb + s, 0)),
                  full(bias_row), slab(fq_slab), slab(fk_slab),
                  full(lcat), full(pq), full(pk), full(cq), full(ck)],
        out_specs=[out_spec, out_spec],
        out_shape=[jax.ShapeDtypeStruct((m, 2 * width), BF16)] * 2,
        scratch_shapes=[pltpu.VMEM((1, LANES), F32)],
        compiler_params=_params("parallel", "arbitrary"),
        name="fox_prep",
    )(ff, bias_row, proj, proj, lcat, pq, pk, cq, ck)


def _rope(t, cos_t, sin_t):
    half = MLA_ROPE // 2
    lane = lax.broadcasted_iota(jnp.int32, t.shape, 1)
    swapped = jnp.where(lane < half,
                        pltpu.roll(t, LANES - half, 1),
                        pltpu.roll(t, half, 1))
    return t * cos_t + swapped * sin_t


def _flash_kernel(qi_ref, kj_ref, *refs, rope_q, ratio):
    if rope_q:
        (q_ref, cos_ref, sin_ref, k_ref, v_ref, o_ref,
         qf_ref, va_ref, m_ref, acc_ref) = refs
    else:
        q_ref, k_ref, v_ref, o_ref, va_ref, m_ref, acc_ref = refs
    tq = q_ref.shape[0]
    tk = k_ref.shape[0]
    p_idx = pl.program_id(2)
    qi = qi_ref[p_idx]
    kj = kj_ref[p_idx]

    @pl.when(kj == 0)
    def _():
        if rope_q:
            qf_ref[:, :HEAD_DIM] = q_ref[:, :HEAD_DIM]
            q2 = _rope(q_ref[:, HEAD_DIM:].astype(F32), cos_ref[...], sin_ref[...])
            qf_ref[:, HEAD_DIM:] = q2.astype(qf_ref.dtype)
        va_ref[:, HEAD_DIM:] = jnp.ones((tk, HEAD_DIM), va_ref.dtype)
        m_ref[...] = jnp.full_like(m_ref, MASK_VALUE)
        acc_ref[...] = jnp.zeros_like(acc_ref)

    def step(masked):
        qsrc = qf_ref if rope_q else q_ref
        va_ref[:, :HEAD_DIM] = v_ref[...]
        rows = tq // FLASH_CHAINS

        def kv_len(r):
            return (r + 1) * rows if (masked and ratio == 1) else tk

        def logits(r):
            return lax.dot_general(qsrc[r * rows:(r + 1) * rows, :],
                                   k_ref[:kv_len(r), :], (((1,), (1,)), ((), ())),
                                   preferred_element_type=F32)

        s_next = logits(0)
        for r in range(FLASH_CHAINS):
            s = s_next
            if r + 1 < FLASH_CHAINS:
                s_next = logits(r + 1)
            sl = slice(r * rows, (r + 1) * rows)
            if masked:
                row = lax.broadcasted_iota(jnp.int32, s.shape, 0) + (qi * tq + r * rows)
                col = lax.broadcasted_iota(jnp.int32, s.shape, 1) + kj * tk
                s = jnp.where(col <= row, s, MASK_VALUE)
            m_prev = m_ref[sl, :]
            m_new = jnp.maximum(m_prev, jnp.max(s, axis=-1, keepdims=True))
            alpha = jnp.exp2(m_prev - m_new)
            p = jnp.exp2(s - jnp.tile(m_new, (1, s.shape[1] // LANES)))
            acc_ref[sl, :] = jnp.tile(alpha, (1, 2)) * acc_ref[sl, :] + jnp.dot(
                p.astype(va_ref.dtype), va_ref[:kv_len(r), :],
                preferred_element_type=F32)
            m_ref[sl, :] = m_new

    @pl.when(kj < qi * ratio)
    def _():
        step(False)

    @pl.when(kj >= qi * ratio)
    def _():
        step(True)

    @pl.when(kj == (qi + 1) * ratio - 1)
    def _():
        acc = acc_ref[...]
        o_ref[...] = (acc[:, :HEAD_DIM] / acc[:, HEAD_DIM:]).astype(o_ref.dtype)


def _flash(q, k, k_off, v, v_off, batch, seq, heads, rope=None):
    m = q.shape[0]
    tq = _pick(seq, (FLASH_TQ, 512, 256, 128))
    tk = _pick(tq, (FLASH_TK, 256, 128))
    nq = seq // tq
    nk = seq // tk
    ratio = tq // tk
    pairs = [(i, j) for i in range(nq) for j in range((i + 1) * ratio)]
    qi_tab = jnp.asarray([p[0] for p in pairs], jnp.int32)
    kj_tab = jnp.asarray([p[1] for p in pairs], jnp.int32)

    in_specs = [pl.BlockSpec((tq, 2 * HEAD_DIM),
                             lambda b, h, p, qi, kj: (b * nq + qi[p], h))]
    args = [q]
    scratch = []
    if rope is not None:
        tab = pl.BlockSpec((tq, LANES), lambda b, h, p, qi, kj: (b * nq + qi[p], 0))
        in_specs += [tab, tab]
        args += list(rope)
        scratch.append(pltpu.VMEM((tq, 2 * HEAD_DIM), BF16))
    in_specs += [pl.BlockSpec((tk, 2 * HEAD_DIM),
                              lambda b, h, p, qi, kj: (b * nk + kj[p], k_off + h)),
                 pl.BlockSpec((tk, HEAD_DIM),
                              lambda b, h, p, qi, kj: (b * nk + kj[p], v_off + h))]
    args += [k, v]
    scratch += [pltpu.VMEM((tk, 2 * HEAD_DIM), BF16),
                pltpu.VMEM((tq, LANES), F32),
                pltpu.VMEM((tq, 2 * HEAD_DIM), F32)]
    return pl.pallas_call(
        functools.partial(_flash_kernel, rope_q=rope is not None, ratio=ratio),
        grid_spec=pltpu.PrefetchScalarGridSpec(
            num_scalar_prefetch=2,
            grid=(batch, heads, len(pairs)),
            in_specs=in_specs,
            out_specs=pl.BlockSpec((tq, HEAD_DIM),
                                   lambda b, h, p, qi, kj: (b * nq + qi[p], h)),
            scratch_shapes=scratch),
        out_shape=jax.ShapeDtypeStruct((m, heads * HEAD_DIM), BF16),
        compiler_params=_params("parallel", "parallel", "arbitrary"),
        name="flash_rope" if rope is not None else "flash",
    )(qi_tab, kj_tab, *args)


def _mla_prep_kernel(mq_ref, mkv_ref, qw_ref, kvw_ref, cos_ref, sin_ref,
                     qn_ref, kvin_ref, *, kv_rank):
    mq = mq_ref[...].astype(F32)
    ms = jnp.mean(mq * mq, axis=-1, keepdims=True)
    qn_ref[...] = (mq * lax.rsqrt(ms + EPS) * qw_ref[...]).astype(qn_ref.dtype)
    lat = mkv_ref[:, :kv_rank].astype(F32)
    ms = jnp.mean(lat * lat, axis=-1, keepdims=True)
    kvin_ref[:, :kv_rank] = (lat * lax.rsqrt(ms + EPS) * kvw_ref[...]).astype(kvin_ref.dtype)
    kpe = mkv_ref[:, kv_rank:kv_rank + LANES].astype(F32)
    kvin_ref[:, kv_rank:] = _rope(kpe, cos_ref[...], sin_ref[...]).astype(kvin_ref.dtype)


def _mla_prep(proj, mq_blk, mkv_blk, q_rank, kv_rank, qw, kvw, cos_t, sin_t):
    m = proj.shape[0]
    tm = _pick(m, (256, 128, 64, 32, 16, 8))
    row = lambda width, blk: pl.BlockSpec((tm, width), lambda i: (i, blk))
    full = lambda a: pl.BlockSpec(a.shape, lambda i: (0, 0))
    return pl.pallas_call(
        functools.partial(_mla_prep_kernel, kv_rank=kv_rank),
        grid=(m // tm,),
        in_specs=[row(q_rank, mq_blk), row(q_rank, mkv_blk), full(qw), full(kvw),
                  row(LANES, 0), row(LANES, 0)],
        out_specs=[row(q_rank, 0), row(kv_rank + LANES, 0)],
        out_shape=[jax.ShapeDtypeStruct((m, q_rank), BF16),
                   jax.ShapeDtypeStruct((m, kv_rank + LANES), BF16)],
        compiler_params=_params("parallel"),
        name="mla_prep",
    )(proj, proj, qw, kvw, cos_t, sin_t)


def _postmix_kernel(ohg_ref, g_ref, ofox_ref, omla_ref, hgw_ref, fw_ref, mw_ref,
                    y_ref, *, hg_w, fox_w):
    def normed(o_ref, w_ref):
        o = o_ref[...].astype(F32)
        ms = jnp.mean(o * o, axis=-1, keepdims=True)
        return o * lax.rsqrt(ms + EPS) * w_ref[...]

    g = g_ref[...].astype(F32)
    y_hg = normed(ohg_ref, hgw_ref) * (g * (1.0 / (1.0 + jnp.exp(-g))))
    y_ref[:, :hg_w] = y_hg.astype(y_ref.dtype)
    y_ref[:, hg_w:hg_w + fox_w] = normed(ofox_ref, fw_ref).astype(y_ref.dtype)
    y_ref[:, hg_w + fox_w:] = normed(omla_ref, mw_ref).astype(y_ref.dtype)


def _postmix(o_hg, proj, g_blk, o_fox, o_mla, hgw, fw, mw):
    m, hg_w = o_hg.shape
    fox_w = o_fox.shape[1]
    mla_w = o_mla.shape[1]
    tm = _pick(m, (256, 128, 64, 32, 16, 8))
    row = lambda width, blk: pl.BlockSpec((tm, width), lambda i: (i, blk))
    full = lambda a: pl.BlockSpec(a.shape, lambda i: (0, 0))
    return pl.pallas_call(
        functools.partial(_postmix_kernel, hg_w=hg_w, fox_w=fox_w),
        grid=(m // tm,),
        in_specs=[row(hg_w, 0), row(hg_w, g_blk), row(fox_w, 0), row(mla_w, 0),
                  full(hgw), full(fw), full(mw)],
        out_specs=row(hg_w + fox_w + mla_w, 0),
        out_shape=jax.ShapeDtypeStruct((m, hg_w + fox_w + mla_w), BF16),
        compiler_params=_params("parallel"),
        name="postmix",
    )(o_hg, proj, o_fox, o_mla, hgw, fw, mw)


def kernel(x, positions, mix_norm, w_in, hg_lower_bounds, hg_out_norm, fox_f_bias,
           fox_out_norm, mla_q_a_norm, mla_w_q_b, mla_kv_a_norm, mla_w_kv_b,
           mla_out_norm, w_o, ffn_norm, w_gate, w_up, w_down, final_norm):
    batch, seq, d_model = x.shape
    depth = w_in.shape[0]
    m = batch * seq
    hg_w = hg_lower_bounds.shape[1]
    hg_heads = hg_w // HEAD_DIM
    fox_heads = fox_f_bias.shape[1]
    fox_w = fox_heads * HEAD_DIM
    q_rank = mla_q_a_norm.shape[1]
    kv_rank = mla_kv_a_norm.shape[1]
    mla_heads = mla_w_q_b.shape[2] // (HEAD_DIM + MLA_ROPE)
    mla_w = mla_heads * HEAD_DIM
    assert kv_rank + LANES <= q_rank and fox_heads <= LANES
    assert hg_w == q_rank and (6 * hg_w) % fox_w == 0

    off_fox = 4 * hg_w
    off_ff = off_fox + 3 * fox_w
    off_mq = off_ff + fox_heads
    off_mkv = off_mq + q_rank
    fox_scale = float(HEAD_DIM) ** -0.5 * LOG2E
    mla_scale = float(HEAD_DIM + MLA_ROPE) ** -0.5 * LOG2E
    pad_kv = q_rank - (kv_rank + MLA_ROPE)
    w_proj = jnp.concatenate([
        w_in[:, :, :off_fox],
        w_in[:, :, off_mq:off_mkv],
        w_in[:, :, off_mkv:],
        jnp.zeros((depth, d_model, pad_kv), w_in.dtype),
        w_in[:, :, off_fox:off_fox + fox_w] * fox_scale,
        w_in[:, :, off_fox + fox_w:off_ff],
    ], axis=2).astype(BF16)
    w_ff = jnp.pad(w_in[:, :, off_ff:off_mq],
                   ((0, 0), (0, 0), (0, LANES - fox_heads))).astype(BF16)
    blk_mq = 4
    blk_mkv = 5
    slab_fq = (6 * hg_w) // fox_w
    blk_fv = (6 * hg_w + 2 * fox_w) // HEAD_DIM

    wq = mla_w_q_b.reshape(depth, q_rank, mla_heads, HEAD_DIM + MLA_ROPE) * mla_scale
    w_qb = jnp.pad(wq, ((0, 0),) * 3 + ((0, HEAD_DIM - MLA_ROPE),)).reshape(
        depth, q_rank, 2 * mla_w).astype(BF16)
    wkv = mla_w_kv_b.reshape(depth, kv_rank, mla_heads, 2 * HEAD_DIM)
    k_cols = jnp.pad(wkv[..., :HEAD_DIM], ((0, 0),) * 3 + ((0, HEAD_DIM),)).reshape(
        depth, kv_rank, 2 * mla_w)
    v_cols = wkv[..., HEAD_DIM:].reshape(depth, kv_rank, mla_w)
    route = np.zeros((LANES, mla_heads, 2 * HEAD_DIM), np.float32)
    route[:, :, HEAD_DIM:] = np.eye(LANES, dtype=np.float32)[:, None, :]
    route = np.concatenate([route.reshape(LANES, 2 * mla_w),
                            np.zeros((LANES, mla_w), np.float32)], axis=1)
    w_kvb = jnp.concatenate([
        jnp.concatenate([k_cols, v_cols], axis=2),
        jnp.broadcast_to(jnp.asarray(route), (depth, LANES, 3 * mla_w)),
    ], axis=1).astype(BF16)
    w_o_b = w_o.astype(BF16)
    w_gate_b = w_gate.astype(BF16)
    w_up_b = w_up.astype(BF16)
    w_down_b = w_down.astype(BF16)

    lb_soft = jax.nn.softmax(hg_lower_bounds.astype(F32), axis=0)
    lb_all = (jnp.cumsum(lb_soft, axis=0) - lb_soft[0]).reshape(depth, 1, hg_w)
    fox_bias = jnp.pad(fox_f_bias.astype(F32),
                       ((0, 0), (0, LANES - fox_heads))).reshape(depth, 1, LANES)

    inv_freq = ROPE_BASE ** (-jnp.arange(0, MLA_ROPE, 2, dtype=F32) / MLA_ROPE)
    angles = positions.astype(F32).reshape(m, 1) * inv_freq
    cos, sin = jnp.cos(angles), jnp.sin(angles)
    zpad = jnp.zeros((m, LANES - MLA_ROPE), F32)
    cos_t = jnp.concatenate([cos, cos, zpad], axis=1)
    sin_t = jnp.concatenate([-sin, sin, zpad], axis=1)

    hg_lcat = _tri_cat(HG_CHUNK)
    xr = x.reshape(m, d_model).astype(F32)

    for l in range(depth):
        h = _rmsnorm(xr, mix_norm[l], BF16)
        proj = _matmul(h, w_proj, l, BF16, 1024, 512, name="in_proj")
        ff = _matmul(h, w_ff, l, F32, 1024, LANES, name="ff_proj")

        o_hg = _hgrn2(proj, lb_all, l, batch, seq, hg_heads, hg_lcat)

        qcat, kcat = _fox_prep(ff, fox_bias[l], proj, slab_fq, slab_fq + 1,
                               batch, seq, fox_heads)
        o_fox = _flash(qcat, kcat, 0, proj, blk_fv, batch, seq, fox_heads)

        qn, kvin = _mla_prep(proj, blk_mq, blk_mkv, q_rank, kv_rank,
                             mla_q_a_norm[l].reshape(1, q_rank).astype(F32),
                             mla_kv_a_norm[l].reshape(1, kv_rank).astype(F32),
                             cos_t, sin_t)
        q_mla = _matmul(qn, w_qb, l, BF16, 1024, 512, name="mla_q_b")
        kv_mla = _matmul(kvin, w_kvb, l, BF16, 1024, 512, name="mla_kv_b")
        o_mla = _flash(q_mla, kv_mla, 0, kv_mla, 2 * mla_heads, batch, seq,
                       mla_heads, rope=(cos_t, sin_t))

        y = _postmix(o_hg, proj, 3, o_fox, o_mla,
                     hg_out_norm[l].reshape(1, hg_w).astype(F32),
                     fox_out_norm[l].reshape(1, fox_w).astype(F32),
                     mla_out_norm[l].reshape(1, mla_w).astype(F32))
        xr = _matmul(y, w_o_b, l, F32, 1024, 512, residual=xr, name="out_proj")

        h = _rmsnorm(xr, ffn_norm[l], BF16)
        act = _swiglu_up(h, w_gate_b, w_up_b, l, 1024, 256)
        xr = _matmul(act, w_down_b, l, F32, 512, 256, residual=xr, name="down_proj")

    out = _rmsnorm(xr, final_norm, x.dtype)
    return out.reshape(batch, seq, d_model)
```

```python
import functools
import math

import numpy as np
import jax
import jax.numpy as jnp
from jax import lax
from jax.experimental import pallas as pl
from jax.experimental.pallas import tpu as pltpu

F32 = jnp.float32
BF16 = jnp.bfloat16

LANES = 128
HEAD_DIM = 128
MLA_ROPE = 64
ROPE_BASE = 10000.0
EPS = 1e-6
MASK_VALUE = -1e30
LOG2E = math.log2(math.e)
HG_CHUNK = 128
HG_SUB = 16
HG_HEADS_PER_STEP = 4
FLASH_TQ = 2048
FLASH_TK = 2048
FLASH_CHAINS = 4
VMEM_LIMIT_BYTES = 52 * 1024 * 1024


def _params(*semantics):
    return pltpu.CompilerParams(dimension_semantics=semantics,
                                vmem_limit_bytes=VMEM_LIMIT_BYTES)


def _pick(n, candidates):
    for c in candidates:
        if n % c == 0:
            return c
    return n


def _rmsnorm_kernel(x_ref, w_ref, o_ref):
    x = x_ref[...].astype(F32)
    ms = jnp.mean(x * x, axis=-1, keepdims=True)
    o_ref[...] = (x * lax.rsqrt(ms + EPS) * w_ref[...]).astype(o_ref.dtype)


def _rmsnorm(x, w, out_dtype):
    m, d = x.shape
    tm = _pick(m, (256, 128, 64, 32, 16, 8))
    return pl.pallas_call(
        _rmsnorm_kernel,
        grid=(m // tm,),
        in_specs=[pl.BlockSpec((tm, d), lambda i: (i, 0)),
                  pl.BlockSpec((1, d), lambda i: (0, 0))],
        out_specs=pl.BlockSpec((tm, d), lambda i: (i, 0)),
        out_shape=jax.ShapeDtypeStruct((m, d), out_dtype),
        compiler_params=_params("parallel"),
        name="rmsnorm",
    )(x, w.reshape(1, d).astype(F32))


def _mm_kernel(a_ref, w_ref, o_ref):
    o_ref[...] = jnp.dot(a_ref[...], w_ref[...],
                         preferred_element_type=F32).astype(o_ref.dtype)


def _mm_res_kernel(a_ref, w_ref, r_ref, o_ref):
    o_ref[...] = r_ref[...] + jnp.dot(a_ref[...], w_ref[...],
                                      preferred_element_type=F32)


def _mm_swiglu_kernel(a_ref, wg_ref, wu_ref, o_ref):
    a = a_ref[...]
    g = jnp.dot(a, wg_ref[...], preferred_element_type=F32)
    u = jnp.dot(a, wu_ref[...], preferred_element_type=F32)
    o_ref[...] = (g * (1.0 / (1.0 + jnp.exp(-g))) * u).astype(o_ref.dtype)


def _matmul(a, w, layer, out_dtype, tm, tn, residual=None, name="matmul"):
    m, k = a.shape
    n = w.shape[-1]
    tm = _pick(m, (tm, 512, 256, 128, 64, 32, 16, 8))
    tn = _pick(n, (tn, 512, 256, 128))
    in_specs = [pl.BlockSpec((tm, k), lambda i, j: (i, 0)),
                pl.BlockSpec((None, k, tn), lambda i, j: (layer, 0, j))]
    args = [a, w]
    kern = _mm_kernel
    if residual is not None:
        in_specs.append(pl.BlockSpec((tm, tn), lambda i, j: (i, j)))
        args.append(residual)
        kern = _mm_res_kernel
    return pl.pallas_call(
        kern,
        grid=(m // tm, n // tn),
        in_specs=in_specs,
        out_specs=pl.BlockSpec((tm, tn), lambda i, j: (i, j)),
        out_shape=jax.ShapeDtypeStruct((m, n), out_dtype),
        compiler_params=_params("parallel", "arbitrary"),
        name=name,
    )(*args)


def _swiglu_up(a, wg, wu, layer, tm, tn):
    m, k = a.shape
    n = wg.shape[-1]
    tm = _pick(m, (tm, 512, 256, 128, 64, 32, 16, 8))
    tn = _pick(n, (tn, 256, 128))
    w_spec = pl.BlockSpec((None, k, tn), lambda i, j: (layer, 0, j))
    return pl.pallas_call(
        _mm_swiglu_kernel,
        grid=(m // tm, n // tn),
        in_specs=[pl.BlockSpec((tm, k), lambda i, j: (i, 0)), w_spec, w_spec],
        out_specs=pl.BlockSpec((tm, tn), lambda i, j: (i, j)),
        out_shape=jax.ShapeDtypeStruct((m, n), BF16),
        compiler_params=_params("parallel", "arbitrary"),
        name="swiglu_up",
    )(a, wg, wu)


def _split3(x):
    hi = x.astype(BF16)
    r1 = x - hi.astype(F32)
    mid = r1.astype(BF16)
    lo = (r1 - mid.astype(F32)).astype(BF16)
    return hi, mid, lo


def _tri_cat(n):
    tri = np.tril(np.ones((n, n), np.float32))
    return jnp.asarray(np.concatenate([tri, tri, tri], axis=1), dtype=BF16)


def _cumsum_rows(x, lcat):
    hi, mid, lo = _split3(x)
    stacked = jnp.concatenate([hi, mid, lo], axis=0)
    return jnp.dot(lcat, stacked, preferred_element_type=F32)


def _hgrn2_diag(qi, ki, bi, base, colid, rowid):
    a = jnp.zeros(colid.shape, F32)
    for s in range(HG_SUB):
        d = jnp.exp2(bi - bi[s:s + 1, :])
        col = jnp.sum(d * (qi * ki[s:s + 1, :]), axis=-1, keepdims=True)
        a = jnp.where(colid == base + s, col, a)
    return jnp.where(colid <= rowid + base, a, 0.0)


def _hgrn2_kernel(q_ref, f_ref, i_ref, lb_ref, lcat_ref, o_ref, st_ref, *,
                  nchunks, k_scale, heads_per_step):
    @pl.when(pl.program_id(2) == 0)
    def _():
        st_ref[...] = jnp.zeros_like(st_ref)

    nsub = HG_CHUNK // HG_SUB
    colid = lax.broadcasted_iota(jnp.int32, (HG_SUB, HG_CHUNK), 1)
    rowid = lax.broadcasted_iota(jnp.int32, (HG_SUB, HG_CHUNK), 0)
    zero_blk = jnp.zeros((HG_SUB, HEAD_DIM), F32)

    def head_chunk(g, r0):
        cols = slice(g * HEAD_DIM, (g + 1) * HEAD_DIM)
        lb = lb_ref[:, cols]
        one_m_lb = 1.0 - lb
        qp = q_ref[pl.ds(r0, HG_CHUNK), cols].astype(F32)
        z = f_ref[pl.ds(r0, HG_CHUNK), cols].astype(F32)
        v = i_ref[pl.ds(r0, HG_CHUNK), cols]
        q = qp * (1.0 / (1.0 + jnp.exp(-qp))) * k_scale
        t = jnp.exp(-jnp.abs(z))
        inv = 1.0 / (1.0 + t)
        pos = z >= 0.0
        sig = jnp.where(pos, inv, t * inv)
        nsig = jnp.where(pos, t * inv, inv)
        kk = one_m_lb * nsig
        lf = jnp.log(lb + one_m_lb * sig)
        b = _cumsum_rows(lf, lcat_ref[...]) * LOG2E
        b_last = b[HG_CHUNK - 1:HG_CHUNK, :]

        st = st_ref[g]
        qd = (q * jnp.exp2(b)).astype(BF16)
        o = lax.dot_general(qd, st.astype(BF16), (((1,), (1,)), ((), ())),
                            preferred_element_type=F32)

        rows = []
        kparts = []
        r_prev = None
        for i in range(nsub):
            lo_r = i * HG_SUB
            bi = b[lo_r:lo_r + HG_SUB, :]
            qi = q[lo_r:lo_r + HG_SUB, :]
            ki = kk[lo_r:lo_r + HG_SUB, :]
            a_i = _hgrn2_diag(qi, ki, bi, lo_r, colid, rowid)
            if i > 0:
                r = b[lo_r - 1:lo_r, :]
                if kparts:
                    fac = jnp.exp2(r - r_prev)
                    kparts = [kp * fac for kp in kparts]
                b_prev = b[lo_r - HG_SUB:lo_r, :]
                kparts.append(kk[lo_r - HG_SUB:lo_r, :] * jnp.exp2(r - b_prev))
                r_prev = r
                kr = jnp.concatenate(kparts + [zero_blk] * (nsub - i), axis=0)
                qr = (qi * jnp.exp2(bi - r)).astype(BF16)
                a_i = a_i + lax.dot_general(qr, kr.astype(BF16),
                                            (((1,), (1,)), ((), ())),
                                            preferred_element_type=F32)
            rows.append(a_i)
        a = jnp.concatenate(rows, axis=0).astype(BF16)
        o = o + jnp.dot(a, v, preferred_element_type=F32)
        o_ref[pl.ds(r0, HG_CHUNK), cols] = o.astype(o_ref.dtype)

        fac = jnp.exp2(b_last - r_prev)
        b_tail = b[HG_CHUNK - HG_SUB:, :]
        kparts = [kp * fac for kp in kparts]
        kparts.append(kk[HG_CHUNK - HG_SUB:, :] * jnp.exp2(b_last - b_tail))
        kd = jnp.concatenate(kparts, axis=0).astype(BF16)
        upd = lax.dot_general(v, kd, (((0,), (0,)), ((), ())),
                              preferred_element_type=F32)
        st_ref[g] = st * jnp.exp2(b_last) + upd

    def chunk(c, carry):
        r0 = pl.multiple_of(c * HG_CHUNK, HG_CHUNK)
        for g in range(heads_per_step):
            head_chunk(g, r0)
        return carry

    lax.fori_loop(0, nchunks, chunk, 0)


def _hgrn2(proj, lb_all, layer, batch, seq, heads, lcat):
    m = proj.shape[0]
    t = _pick(seq, (512, 256, 128))
    nt = seq // t

    hps = _pick(heads, (HG_HEADS_PER_STEP, 1))
    width = hps * HEAD_DIM
    groups = heads // hps

    def col(off):
        return pl.BlockSpec((t, width), lambda b, h, s: (b * nt + s, off + h))

    kern = functools.partial(_hgrn2_kernel, nchunks=t // HG_CHUNK,
                             k_scale=float(HEAD_DIM) ** -0.5, heads_per_step=hps)
    return pl.pallas_call(
        kern,
        grid=(batch, groups, nt),
        in_specs=[col(0), col(groups), col(2 * groups),
                  pl.BlockSpec((None, 1, width), lambda b, h, s: (layer, 0, h)),
                  pl.BlockSpec(lcat.shape, lambda b, h, s: (0, 0))],
        out_specs=pl.BlockSpec((t, width), lambda b, h, s: (b * nt + s, h)),
        out_shape=jax.ShapeDtypeStruct((m, heads * HEAD_DIM), BF16),
        scratch_shapes=[pltpu.VMEM((hps, HEAD_DIM, HEAD_DIM), F32)],
        compiler_params=_params("parallel", "parallel", "arbitrary"),
        name="hgrn2",
    )(proj, proj, proj, lb_all, lcat)


def _fox_prep_kernel(ff_ref, bias_ref, fq_ref, fk_ref, lcat_ref, pq_ref, pk_ref,
                     cq_ref, ck_ref, qcat_ref, kcat_ref, carry_ref, *, heads):
    @pl.when(pl.program_id(1) == 0)
    def _():
        carry_ref[...] = jnp.zeros_like(carry_ref)

    x = ff_ref[...] + bias_ref[...]
    lf = jnp.minimum(x, 0.0) - jnp.log(1.0 + jnp.exp(-jnp.abs(x)))
    c = _cumsum_rows(lf, lcat_ref[...]) + carry_ref[...]
    carry_ref[...] = c[c.shape[0] - 1:, :]
    e = jnp.concatenate(_split3(c * LOG2E), axis=1)
    qe = jnp.dot(e, pq_ref[...], preferred_element_type=F32) + cq_ref[...]
    ke = jnp.dot(e, pk_ref[...], preferred_element_type=F32) + ck_ref[...]
    for h in range(heads):
        src = slice(h * HEAD_DIM, (h + 1) * HEAD_DIM)
        lo = 2 * h * HEAD_DIM
        qcat_ref[:, lo:lo + HEAD_DIM] = fq_ref[:, src]
        qcat_ref[:, lo + HEAD_DIM:lo + 2 * HEAD_DIM] = qe[:, src].astype(qcat_ref.dtype)
        kcat_ref[:, lo:lo + HEAD_DIM] = fk_ref[:, src]
        kcat_ref[:, lo + HEAD_DIM:lo + 2 * HEAD_DIM] = ke[:, src].astype(kcat_ref.dtype)


def _fox_expanders(heads):
    width = heads * HEAD_DIM
    pq = np.zeros((3 * LANES, width), np.float32)
    pk = np.zeros((3 * LANES, width), np.float32)
    cq = np.zeros((1, width), np.float32)
    ck = np.zeros((1, width), np.float32)
    for h in range(heads):
        for part in range(3):
            pq[part * LANES + h, h * HEAD_DIM + part] = 1.0
            pk[part * LANES + h, h * HEAD_DIM + 3 + part] = -1.0
            cq[0, h * HEAD_DIM + 3 + part] = 1.0
            ck[0, h * HEAD_DIM + part] = 1.0
    return (jnp.asarray(pq, BF16), jnp.asarray(pk, BF16),
            jnp.asarray(cq, F32), jnp.asarray(ck, F32))


def _fox_prep(ff, bias_row, proj, fq_slab, fk_slab, batch, seq, heads):
    m = ff.shape[0]
    blk = _pick(seq, (256, 128))
    nb = seq // blk
    width = heads * HEAD_DIM
    lcat = _tri_cat(blk)
    pq, pk, cq, ck = _fox_expanders(heads)
    full = lambda a: pl.BlockSpec(a.shape, lambda b, s: (0, 0))
    slab = lambda cb: pl.BlockSpec((blk, width), lambda b, s: (b * nb + s, cb))
    out_spec = pl.BlockSpec((blk, 2 * width), lambda b, s: (b * nb + s, 0))
    return pl.pallas_call(
        functools.partial(_fox_prep_kernel, heads=heads),
        grid=(batch, nb),
        in_specs=[pl.BlockSpec((blk, LANES), lambda b, s: (b * nb + s, 0)),
                  full(bias_row), slab(fq_slab), slab(fk_slab),
                  full(lcat), full(pq), full(pk), full(cq), full(ck)],
        out_specs=[out_spec, out_spec],
        out_shape=[jax.ShapeDtypeStruct((m, 2 * width), BF16)] * 2,
        scratch_shapes=[pltpu.VMEM((1, LANES), F32)],
        compiler_params=_params("parallel", "arbitrary"),
        name="fox_prep",
    )(ff, bias_row, proj, proj, lcat, pq, pk, cq, ck)


def _rope(t, cos_t, sin_t):
    half = MLA_ROPE // 2
    lane = lax.broadcasted_iota(jnp.int32, t.shape, 1)
    swapped = jnp.where(lane < half,
                        pltpu.roll(t, LANES - half, 1),
                        pltpu.roll(t, half, 1))
    return t * cos_t + swapped * sin_t


def _flash_kernel(qi_ref, kj_ref, *refs, rope_q, ratio):
    if rope_q:
        (q_ref, cos_ref, sin_ref, k_ref, v_ref, o_ref,
         qf_ref, va_ref, m_ref, acc_ref) = refs
    else:
        q_ref, k_ref, v_ref, o_ref, va_ref, m_ref, acc_ref = refs
    tq = q_ref.shape[0]
    tk = k_ref.shape[0]
    p_idx = pl.program_id(2)
    qi = qi_ref[p_idx]
    kj = kj_ref[p_idx]

    @pl.when(kj == 0)
    def _():
        if rope_q:
            qf_ref[:, :HEAD_DIM] = q_ref[:, :HEAD_DIM]
            q2 = _rope(q_ref[:, HEAD_DIM:].astype(F32), cos_ref[...], sin_ref[...])
            qf_ref[:, HEAD_DIM:] = q2.astype(qf_ref.dtype)
        va_ref[:, HEAD_DIM:] = jnp.ones((tk, HEAD_DIM), va_ref.dtype)
        m_ref[...] = jnp.full_like(m_ref, MASK_VALUE)
        acc_ref[...] = jnp.zeros_like(acc_ref)

    def step(masked):
        qsrc = qf_ref if rope_q else q_ref
        va_ref[:, :HEAD_DIM] = v_ref[...]
        rows = tq // FLASH_CHAINS

        def kv_len(r):
            return (r + 1) * rows if (masked and ratio == 1) else tk

        def logits(r):
            return lax.dot_general(qsrc[r * rows:(r + 1) * rows, :],
                                   k_ref[:kv_len(r), :], (((1,), (1,)), ((), ())),
                                   preferred_element_type=F32)

        s_next = logits(0)
        for r in range(FLASH_CHAINS):
            s = s_next
            if r + 1 < FLASH_CHAINS:
                s_next = logits(r + 1)
            sl = slice(r * rows, (r + 1) * rows)
            if masked:
                row = lax.broadcasted_iota(jnp.int32, s.shape, 0) + (qi * tq + r * rows)
                col = lax.broadcasted_iota(jnp.int32, s.shape, 1) + kj * tk
                s = jnp.where(col <= row, s, MASK_VALUE)
            m_prev = m_ref[sl, :]
            m_new = jnp.maximum(m_prev, jnp.max(s, axis=-1, keepdims=True))
            alpha = jnp.exp2(m_prev - m_new)
            p = jnp.exp2(s - jnp.tile(m_new, (1, s.shape[1] // LANES)))
            acc_ref[sl, :] = jnp.tile(alpha, (1, 2)) * acc_ref[sl, :] + jnp.dot(
                p.astype(va_ref.dtype), va_ref[:kv_len(r), :],
                preferred_element_type=F32)
            m_ref[sl, :] = m_new

    @pl.when(kj < qi * ratio)
    def _():
        step(False)

    @pl.when(kj >= qi * ratio)
    def _():
        step(True)

    @pl.when(kj == (qi + 1) * ratio - 1)
    def _():
        acc = acc_ref[...]
        o_ref[...] = (acc[:, :HEAD_DIM] / acc[:, HEAD_DIM:]).astype(o_ref.dtype)


def _flash(q, k, k_off, v, v_off, batch, seq, heads, rope=None):
    m = q.shape[0]
    tq = _pick(seq, (FLASH_TQ, 512, 256, 128))
    tk = _pick(tq, (FLASH_TK, 256, 128))
    nq = seq // tq
    nk = seq // tk
    ratio = tq // tk
    pairs = [(i, j) for i in range(nq) for j in range((i + 1) * ratio)]
    qi_tab = jnp.asarray([p[0] for p in pairs], jnp.int32)
    kj_tab = jnp.asarray([p[1] for p in pairs], jnp.int32)

    in_specs = [pl.BlockSpec((tq, 2 * HEAD_DIM),
                             lambda b, h, p, qi, kj: (b * nq + qi[p], h))]
    args = [q]
    scratch = []
    if rope is not None:
        tab = pl.BlockSpec((tq, LANES), lambda b, h, p, qi, kj: (b * nq + qi[p], 0))
        in_specs += [tab, tab]
        args += list(rope)
        scratch.append(pltpu.VMEM((tq, 2 * HEAD_DIM), BF16))
    in_specs += [pl.BlockSpec((tk, 2 * HEAD_DIM),
                              lambda b, h, p, qi, kj: (b * nk + kj[p], k_off + h)),
                 pl.BlockSpec((tk, HEAD_DIM),
                              lambda b, h, p, qi, kj: (b * nk + kj[p], v_off + h))]
    args += [k, v]
    scratch += [pltpu.VMEM((tk, 2 * HEAD_DIM), BF16),
                pltpu.VMEM((tq, LANES), F32),
                pltpu.VMEM((tq, 2 * HEAD_DIM), F32)]
    return pl.pallas_call(
        functools.partial(_flash_kernel, rope_q=rope is not None, ratio=ratio),
        grid_spec=pltpu.PrefetchScalarGridSpec(
            num_scalar_prefetch=2,
            grid=(batch, heads, len(pairs)),
            in_specs=in_specs,
            out_specs=pl.BlockSpec((tq, HEAD_DIM),
                                   lambda b, h, p, qi, kj: (b * nq + qi[p], h)),
            scratch_shapes=scratch),
        out_shape=jax.ShapeDtypeStruct((m, heads * HEAD_DIM), BF16),
        compiler_params=_params("parallel", "parallel", "arbitrary"),
        name="flash_rope" if rope is not None else "flash",
    )(qi_tab, kj_tab, *args)


def _mla_prep_kernel(mq_ref, mkv_ref, qw_ref, kvw_ref, cos_ref, sin_ref,
                     qn_ref, kvin_ref, *, kv_rank):
    mq = mq_ref[...].astype(F32)
    ms = jnp.mean(mq * mq, axis=-1, keepdims=True)
    qn_ref[...] = (mq * lax.rsqrt(ms + EPS) * qw_ref[...]).astype(qn_ref.dtype)
    lat = mkv_ref[:, :kv_rank].astype(F32)
    ms = jnp.mean(lat * lat, axis=-1, keepdims=True)
    kvin_ref[:, :kv_rank] = (lat * lax.rsqrt(ms + EPS) * kvw_ref[...]).astype(kvin_ref.dtype)
    kpe = mkv_ref[:, kv_rank:kv_rank + LANES].astype(F32)
    kvin_ref[:, kv_rank:] = _rope(kpe, cos_ref[...], sin_ref[...]).astype(kvin_ref.dtype)


def _mla_prep(proj, mq_blk, mkv_blk, q_rank, kv_rank, qw, kvw, cos_t, sin_t):
    m = proj.shape[0]
    tm = _pick(m, (256, 128, 64, 32, 16, 8))
    row = lambda width, blk: pl.BlockSpec((tm, width), lambda i: (i, blk))
    full = lambda a: pl.BlockSpec(a.shape, lambda i: (0, 0))
    return pl.pallas_call(
        functools.partial(_mla_prep_kernel, kv_rank=kv_rank),
        grid=(m // tm,),
        in_specs=[row(q_rank, mq_blk), row(q_rank, mkv_blk), full(qw), full(kvw),
                  row(LANES, 0), row(LANES, 0)],
        out_specs=[row(q_rank, 0), row(kv_rank + LANES, 0)],
        out_shape=[jax.ShapeDtypeStruct((m, q_rank), BF16),
                   jax.ShapeDtypeStruct((m, kv_rank + LANES), BF16)],
        compiler_params=_params("parallel"),
        name="mla_prep",
    )(proj, proj, qw, kvw, cos_t, sin_t)


def _postmix_kernel(ohg_ref, g_ref, ofox_ref, omla_ref, hgw_ref, fw_ref, mw_ref,
                    y_ref, *, hg_w, fox_w):
    def normed(o_ref, w_ref):
        o = o_ref[...].astype(F32)
        ms = jnp.mean(o * o, axis=-1, keepdims=True)
        return o * lax.rsqrt(ms + EPS) * w_ref[...]

    g = g_ref[...].astype(F32)
    y_hg = normed(ohg_ref, hgw_ref) * (g * (1.0 / (1.0 + jnp.exp(-g))))
    y_ref[:, :hg_w] = y_hg.astype(y_ref.dtype)
    y_ref[:, hg_w:hg_w + fox_w] = normed(ofox_ref, fw_ref).astype(y_ref.dtype)
    y_ref[:, hg_w + fox_w:] = normed(omla_ref, mw_ref).astype(y_ref.dtype)


def _postmix(o_hg, proj, g_blk, o_fox, o_mla, hgw, fw, mw):
    m, hg_w = o_hg.shape
    fox_w = o_fox.shape[1]
    mla_w = o_mla.shape[1]
    tm = _pick(m, (256, 128, 64, 32, 16, 8))
    row = lambda width, blk: pl.BlockSpec((tm, width), lambda i: (i, blk))
    full = lambda a: pl.BlockSpec(a.shape, lambda i: (0, 0))
    return pl.pallas_call(
        functools.partial(_postmix_kernel, hg_w=hg_w, fox_w=fox_w),
        grid=(m // tm,),
        in_specs=[row(hg_w, 0), row(hg_w, g_blk), row(fox_w, 0), row(mla_w, 0),
                  full(hgw), full(fw), full(mw)],
        out_specs=row(hg_w + fox_w + mla_w, 0),
        out_shape=jax.ShapeDtypeStruct((m, hg_w + fox_w + mla_w), BF16),
        compiler_params=_params("parallel"),
        name="postmix",
    )(o_hg, proj, o_fox, o_mla, hgw, fw, mw)


def kernel(x, positions, mix_norm, w_in, hg_lower_bounds, hg_out_norm, fox_f_bias,
           fox_out_norm, mla_q_a_norm, mla_w_q_b, mla_kv_a_norm, mla_w_kv_b,
           mla_out_norm, w_o, ffn_norm, w_gate, w_up, w_down, final_norm):
    batch, seq, d_model = x.shape
    depth = w_in.shape[0]
    m = batch * seq
    hg_w = hg_lower_bounds.shape[1]
    hg_heads = hg_w // HEAD_DIM
    fox_heads = fox_f_bias.shape[1]
    fox_w = fox_heads * HEAD_DIM
    q_rank = mla_q_a_norm.shape[1]
    kv_rank = mla_kv_a_norm.shape[1]
    mla_heads = mla_w_q_b.shape[2] // (HEAD_DIM + MLA_ROPE)
    mla_w = mla_heads * HEAD_DIM
    assert kv_rank + LANES <= q_rank and fox_heads <= LANES
    assert hg_w == q_rank and (6 * hg_w) % fox_w == 0

    off_fox = 4 * hg_w
    off_ff = off_fox + 3 * fox_w
    off_mq = off_ff + fox_heads
    off_mkv = off_mq + q_rank
    fox_scale = float(HEAD_DIM) ** -0.5 * LOG2E
    mla_scale = float(HEAD_DIM + MLA_ROPE) ** -0.5 * LOG2E
    pad_kv = q_rank - (kv_rank + MLA_ROPE)
    w_proj = jnp.concatenate([
        w_in[:, :, :off_fox].astype(BF16),
        w_in[:, :, off_mq:off_mkv].astype(BF16),
        w_in[:, :, off_mkv:].astype(BF16),
        jnp.zeros((depth, d_model, pad_kv), BF16),
        (w_in[:, :, off_fox:off_fox + fox_w] * fox_scale).astype(BF16),
        w_in[:, :, off_fox + fox_w:off_ff].astype(BF16),
    ], axis=2)
    w_ff = jnp.pad(w_in[:, :, off_ff:off_mq],
                   ((0, 0), (0, 0), (0, LANES - fox_heads))).astype(BF16)
    blk_mq = 4
    blk_mkv = 5
    slab_fq = (6 * hg_w) // fox_w
    blk_fv = (6 * hg_w + 2 * fox_w) // HEAD_DIM

    wq = mla_w_q_b.reshape(depth, q_rank, mla_heads, HEAD_DIM + MLA_ROPE) * mla_scale
    w_qb = jnp.pad(wq, ((0, 0),) * 3 + ((0, HEAD_DIM - MLA_ROPE),)).reshape(
        depth, q_rank, 2 * mla_w).astype(BF16)
    wkv = mla_w_kv_b.reshape(depth, kv_rank, mla_heads, 2 * HEAD_DIM)
    k_cols = jnp.pad(wkv[..., :HEAD_DIM], ((0, 0),) * 3 + ((0, HEAD_DIM),)).reshape(
        depth, kv_rank, 2 * mla_w)
    v_cols = wkv[..., HEAD_DIM:].reshape(depth, kv_rank, mla_w)
    route = np.zeros((LANES, mla_heads, 2 * HEAD_DIM), np.float32)
    route[:, :, HEAD_DIM:] = np.eye(LANES, dtype=np.float32)[:, None, :]
    route = np.concatenate([route.reshape(LANES, 2 * mla_w),
                            np.zeros((LANES, mla_w), np.float32)], axis=1)
    w_kvb = jnp.concatenate([
        jnp.concatenate([k_cols, v_cols], axis=2),
        jnp.broadcast_to(jnp.asarray(route), (depth, LANES, 3 * mla_w)),
    ], axis=1).astype(BF16)
    w_o_b = w_o.astype(BF16)
    w_gate_b = w_gate.astype(BF16)
    w_up_b = w_up.astype(BF16)
    w_down_b = w_down.astype(BF16)

    lb_soft = jax.nn.softmax(hg_lower_bounds.astype(F32), axis=0)
    lb_all = (jnp.cumsum(lb_soft, axis=0) - lb_soft[0]).reshape(depth, 1, hg_w)
    fox_bias = jnp.pad(fox_f_bias.astype(F32),
                       ((0, 0), (0, LANES - fox_heads))).reshape(depth, 1, LANES)

    inv_freq = ROPE_BASE ** (-jnp.arange(0, MLA_ROPE, 2, dtype=F32) / MLA_ROPE)
    angles = positions.astype(F32).reshape(m, 1) * inv_freq
    cos, sin = jnp.cos(angles), jnp.sin(angles)
    zpad = jnp.zeros((m, LANES - MLA_ROPE), F32)
    cos_t = jnp.concatenate([cos, cos, zpad], axis=1)
    sin_t = jnp.concatenate([-sin, sin, zpad], axis=1)

    hg_lcat = _tri_cat(HG_CHUNK)
    xr = x.reshape(m, d_model).astype(F32)

    for l in range(depth):
        h = _rmsnorm(xr, mix_norm[l], BF16)
        proj = _matmul(h, w_proj, l, BF16, 2048, 512, name="in_proj")
        ff = _matmul(h, w_ff, l, F32, 1024, LANES, name="ff_proj")

        o_hg = _hgrn2(proj, lb_all, l, batch, seq, hg_heads, hg_lcat)

        qcat, kcat = _fox_prep(ff, fox_bias[l], proj, slab_fq, slab_fq + 1,
                               batch, seq, fox_heads)
        o_fox = _flash(qcat, kcat, 0, proj, blk_fv, batch, seq, fox_heads)

        qn, kvin = _mla_prep(proj, blk_mq, blk_mkv, q_rank, kv_rank,
                             mla_q_a_norm[l].reshape(1, q_rank).astype(F32),
                             mla_kv_a_norm[l].reshape(1, kv_rank).astype(F32),
                             cos_t, sin_t)
        q_mla = _matmul(qn, w_qb, l, BF16, 1024, 512, name="mla_q_b")
        kv_mla = _matmul(kvin, w_kvb, l, BF16, 1024, 512, name="mla_kv_b")
        o_mla = _flash(q_mla, kv_mla, 0, kv_mla, 2 * mla_heads, batch, seq,
                       mla_heads, rope=(cos_t, sin_t))

        y = _postmix(o_hg, proj, 3, o_fox, o_mla,
                     hg_out_norm[l].reshape(1, hg_w).astype(F32),
                     fox_out_norm[l].reshape(1, fox_w).astype(F32),
                     mla_out_norm[l].reshape(1, mla_w).astype(F32))
        xr = _matmul(y, w_o_b, l, F32, 1024, 512, residual=xr, name="out_proj")

        h = _rmsnorm(xr, ffn_norm[l], BF16)
        act = _swiglu_up(h, w_gate_b, w_up_b, l, 2048, 256)
        xr = _matmul(act, w_down_b, l, F32, 512, 512, residual=xr, name="down_proj")

    out = _rmsnorm(xr, final_norm, x.dtype)
    return out.reshape(batch, seq, d_model)
```

```python
import functools
import math

import numpy as np
import jax
import jax.numpy as jnp
from jax import lax
from jax.experimental import pallas as pl
from jax.experimental.pallas import tpu as pltpu

F32 = jnp.float32
BF16 = jnp.bfloat16

LANES = 128
HEAD_DIM = 128
MLA_ROPE = 64
ROPE_BASE = 10000.0
EPS = 1e-6
MASK_VALUE = -1e30
LOG2E = math.log2(math.e)
HG_CHUNK = 128
HG_SUB = 16
HG_HEADS_PER_STEP = 4
FLASH_TQ = 2048
FLASH_TK = 2048
FLASH_CHAINS = 4
VMEM_LIMIT_BYTES = 52 * 1024 * 1024


def _params(*semantics):
    return pltpu.CompilerParams(dimension_semantics=semantics,
                                vmem_limit_bytes=VMEM_LIMIT_BYTES)


def _pick(n, candidates):
    for c in candidates:
        if n % c == 0:
            return c
    return n


def _rmsnorm_kernel(x_ref, w_ref, o_ref):
    x = x_ref[...].astype(F32)
    ms = jnp.mean(x * x, axis=-1, keepdims=True)
    o_ref[...] = (x * lax.rsqrt(ms + EPS) * w_ref[...]).astype(o_ref.dtype)


def _rmsnorm(x, w, out_dtype):
    m, d = x.shape
    tm = _pick(m, (256, 128, 64, 32, 16, 8))
    return pl.pallas_call(
        _rmsnorm_kernel,
        grid=(m // tm,),
        in_specs=[pl.BlockSpec((tm, d), lambda i: (i, 0)),
                  pl.BlockSpec((1, d), lambda i: (0, 0))],
        out_specs=pl.BlockSpec((tm, d), lambda i: (i, 0)),
        out_shape=jax.ShapeDtypeStruct((m, d), out_dtype),
        compiler_params=_params("parallel"),
        name="rmsnorm",
    )(x, w.reshape(1, d).astype(F32))


def _rstd_tile(rstd_ref, width):
    return jnp.tile(rstd_ref[...], (1, width // LANES))


def _dot(a, w, trans_w):
    if trans_w:
        return lax.dot_general(a, w, (((1,), (1,)), ((), ())),
                               preferred_element_type=F32)
    return jnp.dot(a, w, preferred_element_type=F32)


def _mm_kernel(a_ref, w_ref, o_ref):
    o_ref[...] = _dot(a_ref[...], w_ref[...], False).astype(o_ref.dtype)


def _mm_norm_kernel(a_ref, rstd_ref, w_ref, o_ref, *, trans_w):
    acc = _dot(a_ref[...], w_ref[...], trans_w)
    o_ref[...] = (acc * _rstd_tile(rstd_ref, acc.shape[1])).astype(o_ref.dtype)


def _mm_swiglu_kernel(a_ref, rstd_ref, wg_ref, wu_ref, o_ref):
    a = a_ref[...]
    rstd = _rstd_tile(rstd_ref, o_ref.shape[1])
    g = jnp.dot(a, wg_ref[...], preferred_element_type=F32) * rstd
    u = jnp.dot(a, wu_ref[...], preferred_element_type=F32) * rstd
    o_ref[...] = (g * (1.0 / (1.0 + jnp.exp(-g))) * u).astype(o_ref.dtype)


def _mm_res_kernel(a_ref, w_ref, r_ref, o_ref, ob_ref, rstd_ref, *, width):
    x = r_ref[...] + jnp.dot(a_ref[...], w_ref[...], preferred_element_type=F32)
    o_ref[...] = x
    ob_ref[...] = x.astype(ob_ref.dtype)
    part = jnp.broadcast_to(jnp.sum(x * x, axis=-1, keepdims=True), rstd_ref.shape)
    j = pl.program_id(1)

    @pl.when(j == 0)
    def _():
        rstd_ref[...] = part

    @pl.when(j > 0)
    def _():
        rstd_ref[...] += part

    @pl.when(j == pl.num_programs(1) - 1)
    def _():
        rstd_ref[...] = lax.rsqrt(rstd_ref[...] * (1.0 / width) + EPS)


def _xprep_kernel(x_ref, xb_ref, rstd_ref):
    x = x_ref[...]
    xb_ref[...] = x.astype(xb_ref.dtype)
    ms = jnp.mean(x * x, axis=-1, keepdims=True)
    rstd_ref[...] = jnp.broadcast_to(lax.rsqrt(ms + EPS), rstd_ref.shape)


def _xprep(x):
    m, d = x.shape
    tm = _pick(m, (256, 128, 64, 32, 16, 8))
    return pl.pallas_call(
        _xprep_kernel,
        grid=(m // tm,),
        in_specs=[pl.BlockSpec((tm, d), lambda i: (i, 0))],
        out_specs=[pl.BlockSpec((tm, d), lambda i: (i, 0)),
                   pl.BlockSpec((tm, LANES), lambda i: (i, 0))],
        out_shape=[jax.ShapeDtypeStruct((m, d), BF16),
                   jax.ShapeDtypeStruct((m, LANES), F32)],
        compiler_params=_params("parallel"),
        name="xprep",
    )(x)


def _row_tile(m, tm):
    return _pick(m, (tm, 512, 256, 128, 64, 32, 16, 8))


def _matmul(a, w, layer, out_dtype, tm, tn, name):
    m, k = a.shape
    n = w.shape[-1]
    tm = _row_tile(m, tm)
    tn = _pick(n, (tn, 512, 256, 128))
    return pl.pallas_call(
        _mm_kernel,
        grid=(m // tm, n // tn),
        in_specs=[pl.BlockSpec((tm, k), lambda i, j: (i, 0)),
                  pl.BlockSpec((None, k, tn), lambda i, j: (layer, 0, j))],
        out_specs=pl.BlockSpec((tm, tn), lambda i, j: (i, j)),
        out_shape=jax.ShapeDtypeStruct((m, n), out_dtype),
        compiler_params=_params("parallel", "arbitrary"),
        name=name,
    )(a, w)


def _matmul_norm(a, rstd, w_t, layer, out_dtype, tm, tn, name):
    m, k = a.shape
    n = w_t.shape[1]
    tm = _row_tile(m, tm)
    tn = _pick(n, (tn, 512, 256, 128))
    return pl.pallas_call(
        functools.partial(_mm_norm_kernel, trans_w=True),
        grid=(m // tm, n // tn),
        in_specs=[pl.BlockSpec((tm, k), lambda i, j: (i, 0)),
                  pl.BlockSpec((tm, LANES), lambda i, j: (i, 0)),
                  pl.BlockSpec((None, tn, k), lambda i, j: (layer, j, 0))],
        out_specs=pl.BlockSpec((tm, tn), lambda i, j: (i, j)),
        out_shape=jax.ShapeDtypeStruct((m, n), out_dtype),
        compiler_params=_params("parallel", "arbitrary"),
        name=name,
    )(a, rstd, w_t)


def _matmul_res(a, w, layer, residual, tm, tn, name):
    m, k = a.shape
    n = w.shape[-1]
    tm = _row_tile(m, tm)
    tn = _pick(n, (tn, 512, 256, 128))
    tile = pl.BlockSpec((tm, tn), lambda i, j: (i, j))
    return pl.pallas_call(
        functools.partial(_mm_res_kernel, width=n),
        grid=(m // tm, n // tn),
        in_specs=[pl.BlockSpec((tm, k), lambda i, j: (i, 0)),
                  pl.BlockSpec((None, k, tn), lambda i, j: (layer, 0, j)),
                  tile],
        out_specs=[tile, tile, pl.BlockSpec((tm, LANES), lambda i, j: (i, 0))],
        out_shape=[jax.ShapeDtypeStruct((m, n), F32),
                   jax.ShapeDtypeStruct((m, n), BF16),
                   jax.ShapeDtypeStruct((m, LANES), F32)],
        compiler_params=_params("parallel", "arbitrary"),
        name=name,
    )(a, w, residual)


def _swiglu_up(a, rstd, wg, wu, layer, tm, tn):
    m, k = a.shape
    n = wg.shape[-1]
    tm = _row_tile(m, tm)
    tn = _pick(n, (tn, 256, 128))
    w_spec = pl.BlockSpec((None, k, tn), lambda i, j: (layer, 0, j))
    return pl.pallas_call(
        _mm_swiglu_kernel,
        grid=(m // tm, n // tn),
        in_specs=[pl.BlockSpec((tm, k), lambda i, j: (i, 0)),
                  pl.BlockSpec((tm, LANES), lambda i, j: (i, 0)), w_spec, w_spec],
        out_specs=pl.BlockSpec((tm, tn), lambda i, j: (i, j)),
        out_shape=jax.ShapeDtypeStruct((m, n), BF16),
        compiler_params=_params("parallel", "arbitrary"),
        name="swiglu_up",
    )(a, rstd, wg, wu)


def _split3(x):
    hi = x.astype(BF16)
    r1 = x - hi.astype(F32)
    mid = r1.astype(BF16)
    lo = (r1 - mid.astype(F32)).astype(BF16)
    return hi, mid, lo


def _tri_cat(n):
    tri = np.tril(np.ones((n, n), np.float32))
    return jnp.asarray(np.concatenate([tri, tri, tri], axis=1), dtype=BF16)


def _cumsum_rows(x, lcat):
    hi, mid, lo = _split3(x)
    stacked = jnp.concatenate([hi, mid, lo], axis=0)
    return jnp.dot(lcat, stacked, preferred_element_type=F32)


def _hgrn2_diag(qi, ki, bi, base, colid, rowid):
    a = jnp.zeros(colid.shape, F32)
    for s in range(HG_SUB):
        d = jnp.exp2(bi - bi[s:s + 1, :])
        col = jnp.sum(d * (qi * ki[s:s + 1, :]), axis=-1, keepdims=True)
        a = jnp.where(colid == base + s, col, a)
    return jnp.where(colid <= rowid + base, a, 0.0)


def _hgrn2_kernel(q_ref, f_ref, i_ref, lb_ref, lcat_ref, o_ref, st_ref, *,
                  nchunks, k_scale, heads_per_step):
    @pl.when(pl.program_id(2) == 0)
    def _():
        st_ref[...] = jnp.zeros_like(st_ref)

    nsub = HG_CHUNK // HG_SUB
    colid = lax.broadcasted_iota(jnp.int32, (HG_SUB, HG_CHUNK), 1)
    rowid = lax.broadcasted_iota(jnp.int32, (HG_SUB, HG_CHUNK), 0)
    zero_blk = jnp.zeros((HG_SUB, HEAD_DIM), F32)

    def head_chunk(g, r0):
        cols = slice(g * HEAD_DIM, (g + 1) * HEAD_DIM)
        lb = lb_ref[:, cols]
        one_m_lb = 1.0 - lb
        qp = q_ref[pl.ds(r0, HG_CHUNK), cols].astype(F32)
        z = f_ref[pl.ds(r0, HG_CHUNK), cols].astype(F32)
        v = i_ref[pl.ds(r0, HG_CHUNK), cols]
        q = qp * (1.0 / (1.0 + jnp.exp(-qp))) * k_scale
        t = jnp.exp(-jnp.abs(z))
        inv = 1.0 / (1.0 + t)
        pos = z >= 0.0
        sig = jnp.where(pos, inv, t * inv)
        nsig = jnp.where(pos, t * inv, inv)
        kk = one_m_lb * nsig
        lf = jnp.log(lb + one_m_lb * sig)
        b = _cumsum_rows(lf, lcat_ref[...]) * LOG2E
        b_last = b[HG_CHUNK - 1:HG_CHUNK, :]

        st = st_ref[g]
        qd = (q * jnp.exp2(b)).astype(BF16)
        o = lax.dot_general(qd, st.astype(BF16), (((1,), (1,)), ((), ())),
                            preferred_element_type=F32)

        rows = []
        kparts = []
        r_prev = None
        for i in range(nsub):
            lo_r = i * HG_SUB
            bi = b[lo_r:lo_r + HG_SUB, :]
            qi = q[lo_r:lo_r + HG_SUB, :]
            ki = kk[lo_r:lo_r + HG_SUB, :]
            a_i = _hgrn2_diag(qi, ki, bi, lo_r, colid, rowid)
            if i > 0:
                r = b[lo_r - 1:lo_r, :]
                if kparts:
                    fac = jnp.exp2(r - r_prev)
                    kparts = [kp * fac for kp in kparts]
                b_prev = b[lo_r - HG_SUB:lo_r, :]
                kparts.append(kk[lo_r - HG_SUB:lo_r, :] * jnp.exp2(r - b_prev))
                r_prev = r
                kr = jnp.concatenate(kparts + [zero_blk] * (nsub - i), axis=0)
                qr = (qi * jnp.exp2(bi - r)).astype(BF16)
                a_i = a_i + lax.dot_general(qr, kr.astype(BF16),
                                            (((1,), (1,)), ((), ())),
                                            preferred_element_type=F32)
            rows.append(a_i)
        a = jnp.concatenate(rows, axis=0).astype(BF16)
        o = o + jnp.dot(a, v, preferred_element_type=F32)
        o_ref[pl.ds(r0, HG_CHUNK), cols] = o.astype(o_ref.dtype)

        fac = jnp.exp2(b_last - r_prev)
        b_tail = b[HG_CHUNK - HG_SUB:, :]
        kparts = [kp * fac for kp in kparts]
        kparts.append(kk[HG_CHUNK - HG_SUB:, :] * jnp.exp2(b_last - b_tail))
        kd = jnp.concatenate(kparts, axis=0).astype(BF16)
        upd = lax.dot_general(v, kd, (((0,), (0,)), ((), ())),
                              preferred_element_type=F32)
        st_ref[g] = st * jnp.exp2(b_last) + upd

    def chunk(c, carry):
        r0 = pl.multiple_of(c * HG_CHUNK, HG_CHUNK)
        for g in range(heads_per_step):
            head_chunk(g, r0)
        return carry

    lax.fori_loop(0, nchunks, chunk, 0)


def _hgrn2(proj, lb_all, layer, batch, seq, heads, lcat):
    m = proj.shape[0]
    t = _pick(seq, (512, 256, 128))
    nt = seq // t

    hps = _pick(heads, (HG_HEADS_PER_STEP, 1))
    width = hps * HEAD_DIM
    groups = heads // hps

    def col(off):
        return pl.BlockSpec((t, width), lambda b, h, s: (b * nt + s, off + h))

    kern = functools.partial(_hgrn2_kernel, nchunks=t // HG_CHUNK,
                             k_scale=float(HEAD_DIM) ** -0.5, heads_per_step=hps)
    return pl.pallas_call(
        kern,
        grid=(batch, groups, nt),
        in_specs=[col(0), col(groups), col(2 * groups),
                  pl.BlockSpec((None, 1, width), lambda b, h, s: (layer, 0, h)),
                  pl.BlockSpec(lcat.shape, lambda b, h, s: (0, 0))],
        out_specs=pl.BlockSpec((t, width), lambda b, h, s: (b * nt + s, h)),
        out_shape=jax.ShapeDtypeStruct((m, heads * HEAD_DIM), BF16),
        scratch_shapes=[pltpu.VMEM((hps, HEAD_DIM, HEAD_DIM), F32)],
        compiler_params=_params("parallel", "parallel", "arbitrary"),
        name="hgrn2",
    )(proj, proj, proj, lb_all, lcat)


def _fox_prep_kernel(ff_ref, bias_ref, fq_ref, fk_ref, lcat_ref, pq_ref, pk_ref,
                     cq_ref, ck_ref, qcat_ref, kcat_ref, carry_ref, *, heads):
    @pl.when(pl.program_id(1) == 0)
    def _():
        carry_ref[...] = jnp.zeros_like(carry_ref)

    x = ff_ref[...] + bias_ref[...]
    lf = jnp.minimum(x, 0.0) - jnp.log(1.0 + jnp.exp(-jnp.abs(x)))
    c = _cumsum_rows(lf, lcat_ref[...]) + carry_ref[...]
    carry_ref[...] = c[c.shape[0] - 1:, :]
    e = jnp.concatenate(_split3(c * LOG2E), axis=1)
    qe = jnp.dot(e, pq_ref[...], preferred_element_type=F32) + cq_ref[...]
    ke = jnp.dot(e, pk_ref[...], preferred_element_type=F32) + ck_ref[...]
    for h in range(heads):
        src = slice(h * HEAD_DIM, (h + 1) * HEAD_DIM)
        lo = 2 * h * HEAD_DIM
        qcat_ref[:, lo:lo + HEAD_DIM] = fq_ref[:, src]
        qcat_ref[:, lo + HEAD_DIM:lo + 2 * HEAD_DIM] = qe[:, src].astype(qcat_ref.dtype)
        kcat_ref[:, lo:lo + HEAD_DIM] = fk_ref[:, src]
        kcat_ref[:, lo + HEAD_DIM:lo + 2 * HEAD_DIM] = ke[:, src].astype(kcat_ref.dtype)


def _fox_expanders(heads):
    width = heads * HEAD_DIM
    pq = np.zeros((3 * LANES, width), np.float32)
    pk = np.zeros((3 * LANES, width), np.float32)
    cq = np.zeros((1, width), np.float32)
    ck = np.zeros((1, width), np.float32)
    for h in range(heads):
        for part in range(3):
            pq[part * LANES + h, h * HEAD_DIM + part] = 1.0
            pk[part * LANES + h, h * HEAD_DIM + 3 + part] = -1.0
            cq[0, h * HEAD_DIM + 3 + part] = 1.0
            ck[0, h * HEAD_DIM + part] = 1.0
    return (jnp.asarray(pq, BF16), jnp.asarray(pk, BF16),
            jnp.asarray(cq, F32), jnp.asarray(ck, F32))


def _fox_prep(ff, bias_row, proj, fq_slab, fk_slab, batch, seq, heads):
    m = ff.shape[0]
    blk = _pick(seq, (256, 128))
    nb = seq // blk
    width = heads * HEAD_DIM
    lcat = _tri_cat(blk)
    pq, pk, cq, ck = _fox_expanders(heads)
    full = lambda a: pl.BlockSpec(a.shape, lambda b, s: (0, 0))
    slab = lambda cb: pl.BlockSpec((blk, width), lambda b, s: (b * nb + s, cb))
    out_spec = pl.BlockSpec((blk, 2 * width), lambda b, s: (b * nb + s, 0))
    return pl.pallas_call(
        functools.partial(_fox_prep_kernel, heads=heads),
        grid=(batch, nb),
        in_specs=[pl.BlockSpec((blk, LANES), lambda b, s: (b * nb + s, 0)),
                  full(bias_row), slab(fq_slab), slab(fk_slab),
                  full(lcat), full(pq), full(pk), full(cq), full(ck)],
        out_specs=[out_spec, out_spec],
        out_shape=[jax.ShapeDtypeStruct((m, 2 * width), BF16)] * 2,
        scratch_shapes=[pltpu.VMEM((1, LANES), F32)],
        compiler_params=_params("parallel", "arbitrary"),
        name="fox_prep",
    )(ff, bias_row, proj, proj, lcat, pq, pk, cq, ck)


def _rope(t, cos_t, sin_t):
    half = MLA_ROPE // 2
    lane = lax.broadcasted_iota(jnp.int32, t.shape, 1)
    swapped = jnp.where(lane < half,
                        pltpu.roll(t, LANES - half, 1),
                        pltpu.roll(t, half, 1))
    return t * cos_t + swapped * sin_t


def _flash_kernel(qi_ref, kj_ref, *refs, rope_q, ratio):
    if rope_q:
        (q_ref, cos_ref, sin_ref, k_ref, v_ref, o_ref,
         qf_ref, va_ref, m_ref, acc_ref) = refs
    else:
        q_ref, k_ref, v_ref, o_ref, va_ref, m_ref, acc_ref = refs
    tq = q_ref.shape[0]
    tk = k_ref.shape[0]
    p_idx = pl.program_id(2)
    qi = qi_ref[p_idx]
    kj = kj_ref[p_idx]

    @pl.when(kj == 0)
    def _():
        if rope_q:
            qf_ref[:, :HEAD_DIM] = q_ref[:, :HEAD_DIM]
            q2 = _rope(q_ref[:, HEAD_DIM:].astype(F32), cos_ref[...], sin_ref[...])
            qf_ref[:, HEAD_DIM:] = q2.astype(qf_ref.dtype)
        va_ref[:, HEAD_DIM:] = jnp.ones((tk, HEAD_DIM), va_ref.dtype)
        m_ref[...] = jnp.full_like(m_ref, MASK_VALUE)
        acc_ref[...] = jnp.zeros_like(acc_ref)

    def step(masked):
        qsrc = qf_ref if rope_q else q_ref
        va_ref[:, :HEAD_DIM] = v_ref[...]
        rows = tq // FLASH_CHAINS

        def kv_len(r):
            return (r + 1) * rows if (masked and ratio == 1) else tk

        def logits(r):
            return lax.dot_general(qsrc[r * rows:(r + 1) * rows, :],
                                   k_ref[:kv_len(r), :], (((1,), (1,)), ((), ())),
                                   preferred_element_type=F32)

        s_next = logits(0)
        for r in range(FLASH_CHAINS):
            s = s_next
            if r + 1 < FLASH_CHAINS:
                s_next = logits(r + 1)
            sl = slice(r * rows, (r + 1) * rows)
            if masked and ratio == 1:
                tri = (lax.broadcasted_iota(jnp.int32, (rows, rows), 1)
                       <= lax.broadcasted_iota(jnp.int32, (rows, rows), 0))
                s_diag = jnp.where(tri, s[:, r * rows:], MASK_VALUE)
                s = s_diag if r == 0 else jnp.concatenate([s[:, :r * rows], s_diag], axis=1)
            elif masked:
                row = lax.broadcasted_iota(jnp.int32, s.shape, 0) + (qi * tq + r * rows)
                col = lax.broadcasted_iota(jnp.int32, s.shape, 1) + kj * tk
                s = jnp.where(col <= row, s, MASK_VALUE)
            m_prev = m_ref[sl, :]
            m_new = jnp.maximum(m_prev, jnp.max(s, axis=-1, keepdims=True))
            alpha = jnp.exp2(m_prev - m_new)
            p = jnp.exp2(s - jnp.tile(m_new, (1, s.shape[1] // LANES)))
            acc_ref[sl, :] = jnp.tile(alpha, (1, 2)) * acc_ref[sl, :] + jnp.dot(
                p.astype(va_ref.dtype), va_ref[:kv_len(r), :],
                preferred_element_type=F32)
            m_ref[sl, :] = m_new

    @pl.when(kj < qi * ratio)
    def _():
        step(False)

    @pl.when(kj >= qi * ratio)
    def _():
        step(True)

    @pl.when(kj == (qi + 1) * ratio - 1)
    def _():
        acc = acc_ref[...]
        o_ref[...] = (acc[:, :HEAD_DIM] / acc[:, HEAD_DIM:]).astype(o_ref.dtype)


def _flash(q, k, k_off, v, v_off, batch, seq, heads, rope=None):
    m = q.shape[0]
    tq = _pick(seq, (FLASH_TQ, 512, 256, 128))
    tk = _pick(tq, (FLASH_TK, 256, 128))
    nq = seq // tq
    nk = seq // tk
    ratio = tq // tk
    pairs = [(i, j) for i in range(nq) for j in range((i + 1) * ratio)]
    qi_tab = jnp.asarray([p[0] for p in pairs], jnp.int32)
    kj_tab = jnp.asarray([p[1] for p in pairs], jnp.int32)

    in_specs = [pl.BlockSpec((tq, 2 * HEAD_DIM),
                             lambda b, h, p, qi, kj: (b * nq + qi[p], h))]
    args = [q]
    scratch = []
    if rope is not None:
        tab = pl.BlockSpec((tq, LANES), lambda b, h, p, qi, kj: (b * nq + qi[p], 0))
        in_specs += [tab, tab]
        args += list(rope)
        scratch.append(pltpu.VMEM((tq, 2 * HEAD_DIM), BF16))
    in_specs += [pl.BlockSpec((tk, 2 * HEAD_DIM),
                              lambda b, h, p, qi, kj: (b * nk + kj[p], k_off + h)),
                 pl.BlockSpec((tk, HEAD_DIM),
                              lambda b, h, p, qi, kj: (b * nk + kj[p], v_off + h))]
    args += [k, v]
    scratch += [pltpu.VMEM((tk, 2 * HEAD_DIM), BF16),
                pltpu.VMEM((tq, LANES), F32),
                pltpu.VMEM((tq, 2 * HEAD_DIM), F32)]
    return pl.pallas_call(
        functools.partial(_flash_kernel, rope_q=rope is not None, ratio=ratio),
        grid_spec=pltpu.PrefetchScalarGridSpec(
            num_scalar_prefetch=2,
            grid=(batch, heads, len(pairs)),
            in_specs=in_specs,
            out_specs=pl.BlockSpec((tq, HEAD_DIM),
                                   lambda b, h, p, qi, kj: (b * nq + qi[p], h)),
            scratch_shapes=scratch),
        out_shape=jax.ShapeDtypeStruct((m, heads * HEAD_DIM), BF16),
        compiler_params=_params("parallel", "parallel", "arbitrary"),
        name="flash_rope" if rope is not None else "flash",
    )(qi_tab, kj_tab, *args)


def _mla_prep_kernel(mq_ref, mkv_ref, qw_ref, kvw_ref, cos_ref, sin_ref,
                     qn_ref, kvin_ref, *, kv_rank):
    mq = mq_ref[...].astype(F32)
    ms = jnp.mean(mq * mq, axis=-1, keepdims=True)
    qn_ref[...] = (mq * lax.rsqrt(ms + EPS) * qw_ref[...]).astype(qn_ref.dtype)
    lat = mkv_ref[:, :kv_rank].astype(F32)
    ms = jnp.mean(lat * lat, axis=-1, keepdims=True)
    kvin_ref[:, :kv_rank] = (lat * lax.rsqrt(ms + EPS) * kvw_ref[...]).astype(kvin_ref.dtype)
    kpe = mkv_ref[:, kv_rank:kv_rank + LANES].astype(F32)
    kvin_ref[:, kv_rank:] = _rope(kpe, cos_ref[...], sin_ref[...]).astype(kvin_ref.dtype)


def _mla_prep(proj, mq_blk, mkv_blk, q_rank, kv_rank, qw, kvw, cos_t, sin_t):
    m = proj.shape[0]
    tm = _pick(m, (256, 128, 64, 32, 16, 8))
    row = lambda width, blk: pl.BlockSpec((tm, width), lambda i: (i, blk))
    full = lambda a: pl.BlockSpec(a.shape, lambda i: (0, 0))
    return pl.pallas_call(
        functools.partial(_mla_prep_kernel, kv_rank=kv_rank),
        grid=(m // tm,),
        in_specs=[row(q_rank, mq_blk), row(q_rank, mkv_blk), full(qw), full(kvw),
                  row(LANES, 0), row(LANES, 0)],
        out_specs=[row(q_rank, 0), row(kv_rank + LANES, 0)],
        out_shape=[jax.ShapeDtypeStruct((m, q_rank), BF16),
                   jax.ShapeDtypeStruct((m, kv_rank + LANES), BF16)],
        compiler_params=_params("parallel"),
        name="mla_prep",
    )(proj, proj, qw, kvw, cos_t, sin_t)


def _postmix_kernel(ohg_ref, g_ref, ofox_ref, omla_ref, hgw_ref, fw_ref, mw_ref,
                    y_ref, *, hg_w, fox_w):
    def normed(o_ref, w_ref):
        o = o_ref[...].astype(F32)
        ms = jnp.mean(o * o, axis=-1, keepdims=True)
        return o * lax.rsqrt(ms + EPS) * w_ref[...]

    g = g_ref[...].astype(F32)
    y_hg = normed(ohg_ref, hgw_ref) * (g * (1.0 / (1.0 + jnp.exp(-g))))
    y_ref[:, :hg_w] = y_hg.astype(y_ref.dtype)
    y_ref[:, hg_w:hg_w + fox_w] = normed(ofox_ref, fw_ref).astype(y_ref.dtype)
    y_ref[:, hg_w + fox_w:] = normed(omla_ref, mw_ref).astype(y_ref.dtype)


def _postmix(o_hg, proj, g_blk, o_fox, o_mla, hgw, fw, mw):
    m, hg_w = o_hg.shape
    fox_w = o_fox.shape[1]
    mla_w = o_mla.shape[1]
    tm = _pick(m, (256, 128, 64, 32, 16, 8))
    row = lambda width, blk: pl.BlockSpec((tm, width), lambda i: (i, blk))
    full = lambda a: pl.BlockSpec(a.shape, lambda i: (0, 0))
    return pl.pallas_call(
        functools.partial(_postmix_kernel, hg_w=hg_w, fox_w=fox_w),
        grid=(m // tm,),
        in_specs=[row(hg_w, 0), row(hg_w, g_blk), row(fox_w, 0), row(mla_w, 0),
                  full(hgw), full(fw), full(mw)],
        out_specs=row(hg_w + fox_w + mla_w, 0),
        out_shape=jax.ShapeDtypeStruct((m, hg_w + fox_w + mla_w), BF16),
        compiler_params=_params("parallel"),
        name="postmix",
    )(o_hg, proj, o_fox, o_mla, hgw, fw, mw)


def kernel(x, positions, mix_norm, w_in, hg_lower_bounds, hg_out_norm, fox_f_bias,
           fox_out_norm, mla_q_a_norm, mla_w_q_b, mla_kv_a_norm, mla_w_kv_b,
           mla_out_norm, w_o, ffn_norm, w_gate, w_up, w_down, final_norm):
    batch, seq, d_model = x.shape
    depth = w_in.shape[0]
    m = batch * seq
    hg_w = hg_lower_bounds.shape[1]
    hg_heads = hg_w // HEAD_DIM
    fox_heads = fox_f_bias.shape[1]
    fox_w = fox_heads * HEAD_DIM
    q_rank = mla_q_a_norm.shape[1]
    kv_rank = mla_kv_a_norm.shape[1]
    mla_heads = mla_w_q_b.shape[2] // (HEAD_DIM + MLA_ROPE)
    mla_w = mla_heads * HEAD_DIM
    assert kv_rank + LANES <= q_rank and fox_heads <= LANES
    assert hg_w == q_rank and (6 * hg_w) % fox_w == 0

    off_fox = 4 * hg_w
    off_ff = off_fox + 3 * fox_w
    off_mq = off_ff + fox_heads
    off_mkv = off_mq + q_rank
    fox_scale = float(HEAD_DIM) ** -0.5 * LOG2E
    mla_scale = float(HEAD_DIM + MLA_ROPE) ** -0.5 * LOG2E
    pad_kv = q_rank - (kv_rank + MLA_ROPE)
    w_in_t = jnp.transpose(w_in, (0, 2, 1)) * mix_norm.astype(F32)[:, None, :]
    w_proj = jnp.concatenate([
        w_in_t[:, :off_fox].astype(BF16),
        w_in_t[:, off_mq:off_mkv].astype(BF16),
        w_in_t[:, off_mkv:].astype(BF16),
        jnp.zeros((depth, pad_kv, d_model), BF16),
        (w_in_t[:, off_fox:off_fox + fox_w] * fox_scale).astype(BF16),
        w_in_t[:, off_fox + fox_w:off_ff].astype(BF16),
    ], axis=1)
    w_ff = jnp.pad(w_in_t[:, off_ff:off_mq],
                   ((0, 0), (0, LANES - fox_heads), (0, 0))).astype(BF16)
    blk_mq = 4
    blk_mkv = 5
    slab_fq = (6 * hg_w) // fox_w
    blk_fv = (6 * hg_w + 2 * fox_w) // HEAD_DIM

    wq = mla_w_q_b.reshape(depth, q_rank, mla_heads, HEAD_DIM + MLA_ROPE) * mla_scale
    w_qb = jnp.pad(wq, ((0, 0),) * 3 + ((0, HEAD_DIM - MLA_ROPE),)).reshape(
        depth, q_rank, 2 * mla_w).astype(BF16)
    wkv = mla_w_kv_b.reshape(depth, kv_rank, mla_heads, 2 * HEAD_DIM)
    k_cols = jnp.pad(wkv[..., :HEAD_DIM], ((0, 0),) * 3 + ((0, HEAD_DIM),)).reshape(
        depth, kv_rank, 2 * mla_w)
    v_cols = wkv[..., HEAD_DIM:].reshape(depth, kv_rank, mla_w)
    route = np.zeros((LANES, mla_heads, 2 * HEAD_DIM), np.float32)
    route[:, :, HEAD_DIM:] = np.eye(LANES, dtype=np.float32)[:, None, :]
    route = np.concatenate([route.reshape(LANES, 2 * mla_w),
                            np.zeros((LANES, mla_w), np.float32)], axis=1)
    w_kvb = jnp.concatenate([
        jnp.concatenate([k_cols, v_cols], axis=2),
        jnp.broadcast_to(jnp.asarray(route), (depth, LANES, 3 * mla_w)),
    ], axis=1).astype(BF16)
    w_o_b = w_o.astype(BF16)
    ffn_gain = ffn_norm.astype(F32)[:, :, None]
    w_gate_b = (w_gate * ffn_gain).astype(BF16)
    w_up_b = (w_up * ffn_gain).astype(BF16)
    w_down_b = w_down.astype(BF16)

    lb_soft = jax.nn.softmax(hg_lower_bounds.astype(F32), axis=0)
    lb_all = (jnp.cumsum(lb_soft, axis=0) - lb_soft[0]).reshape(depth, 1, hg_w)
    fox_bias = jnp.pad(fox_f_bias.astype(F32),
                       ((0, 0), (0, LANES - fox_heads))).reshape(depth, 1, LANES)

    inv_freq = ROPE_BASE ** (-jnp.arange(0, MLA_ROPE, 2, dtype=F32) / MLA_ROPE)
    angles = positions.astype(F32).reshape(m, 1) * inv_freq
    cos, sin = jnp.cos(angles), jnp.sin(angles)
    zpad = jnp.zeros((m, LANES - MLA_ROPE), F32)
    cos_t = jnp.concatenate([cos, cos, zpad], axis=1)
    sin_t = jnp.concatenate([-sin, sin, zpad], axis=1)

    hg_lcat = _tri_cat(HG_CHUNK)
    xr = x.reshape(m, d_model).astype(F32)
    xb, rstd = _xprep(xr)

    for l in range(depth):
        proj = _matmul_norm(xb, rstd, w_proj, l, BF16, 2048, 512, name="in_proj")
        ff = _matmul_norm(xb, rstd, w_ff, l, F32, 1024, LANES, name="ff_proj")

        o_hg = _hgrn2(proj, lb_all, l, batch, seq, hg_heads, hg_lcat)

        qcat, kcat = _fox_prep(ff, fox_bias[l], proj, slab_fq, slab_fq + 1,
                               batch, seq, fox_heads)
        o_fox = _flash(qcat, kcat, 0, proj, blk_fv, batch, seq, fox_heads)

        qn, kvin = _mla_prep(proj, blk_mq, blk_mkv, q_rank, kv_rank,
                             mla_q_a_norm[l].reshape(1, q_rank).astype(F32),
                             mla_kv_a_norm[l].reshape(1, kv_rank).astype(F32),
                             cos_t, sin_t)
        q_mla = _matmul(qn, w_qb, l, BF16, 1024, 512, name="mla_q_b")
        kv_mla = _matmul(kvin, w_kvb, l, BF16, 1024, 512, name="mla_kv_b")
        o_mla = _flash(q_mla, kv_mla, 0, kv_mla, 2 * mla_heads, batch, seq,
                       mla_heads, rope=(cos_t, sin_t))

        y = _postmix(o_hg, proj, 3, o_fox, o_mla,
                     hg_out_norm[l].reshape(1, hg_w).astype(F32),
                     fox_out_norm[l].reshape(1, fox_w).astype(F32),
                     mla_out_norm[l].reshape(1, mla_w).astype(F32))
        xr, xb, rstd = _matmul_res(y, w_o_b, l, xr, 1024, 512, name="out_proj")

        act = _swiglu_up(xb, rstd, w_gate_b, w_up_b, l, 2048, 256)
        xr, xb, rstd = _matmul_res(act, w_down_b, l, xr, 512, 512, name="down_proj")

    out = _rmsnorm(xr, final_norm, x.dtype)
    return out.reshape(batch, seq, d_model)
```

```python
import functools
import math

import numpy as np
import jax
import jax.numpy as jnp
from jax import lax
from jax.experimental import pallas as pl
from jax.experimental.pallas import tpu as pltpu

F32 = jnp.float32
BF16 = jnp.bfloat16

LANES = 128
HEAD_DIM = 128
MLA_ROPE = 64
ROPE_BASE = 10000.0
EPS = 1e-6
MASK_VALUE = -1e30
LOG2E = math.log2(math.e)
HG_CHUNK = 128
HG_SUB = 16
HG_HEADS_PER_STEP = 4
FLASH_TQ = 2048
FLASH_TK = 2048
MM_CHUNKS = 2
FLASH_CHAINS = 4
FLASH_KV_SPLIT = 1
VMEM_LIMIT_BYTES = 52 * 1024 * 1024


def _params(*semantics):
    return pltpu.CompilerParams(dimension_semantics=semantics,
                                vmem_limit_bytes=VMEM_LIMIT_BYTES)


def _pick(n, candidates):
    for c in candidates:
        if n % c == 0:
            return c
    return n


def _rmsnorm_kernel(x_ref, w_ref, o_ref):
    x = x_ref[...].astype(F32)
    ms = jnp.mean(x * x, axis=-1, keepdims=True)
    o_ref[...] = (x * lax.rsqrt(ms + EPS) * w_ref[...]).astype(o_ref.dtype)


def _rmsnorm(x, w, out_dtype):
    m, d = x.shape
    tm = _pick(m, (256, 128, 64, 32, 16, 8))
    return pl.pallas_call(
        _rmsnorm_kernel,
        grid=(m // tm,),
        in_specs=[pl.BlockSpec((tm, d), lambda i: (i, 0)),
                  pl.BlockSpec((1, d), lambda i: (0, 0))],
        out_specs=pl.BlockSpec((tm, d), lambda i: (i, 0)),
        out_shape=jax.ShapeDtypeStruct((m, d), out_dtype),
        compiler_params=_params("parallel"),
        name="rmsnorm",
    )(x, w.reshape(1, d).astype(F32))


def _rstd_tile(rstd_ref, width):
    return jnp.tile(rstd_ref[...], (1, width // LANES))


def _dot(a, w, trans_w):
    if trans_w:
        return lax.dot_general(a, w, (((1,), (1,)), ((), ())),
                               preferred_element_type=F32)
    return jnp.dot(a, w, preferred_element_type=F32)


def _mm_kernel(a_ref, w_ref, o_ref):
    o_ref[...] = _dot(a_ref[...], w_ref[...], False).astype(o_ref.dtype)


def _mm_norm_kernel(a_ref, rstd_ref, w_ref, o_ref, *, trans_w):
    acc = _dot(a_ref[...], w_ref[...], trans_w)
    o_ref[...] = (acc * _rstd_tile(rstd_ref, acc.shape[1])).astype(o_ref.dtype)


def _mm_swiglu_kernel(a_ref, rstd_ref, wg_ref, wu_ref, o_ref):
    rows = a_ref.shape[0] // MM_CHUNKS
    wg = wg_ref[...].astype(a_ref.dtype)
    wu = wu_ref[...].astype(a_ref.dtype)

    def dots(r):
        a = a_ref[r * rows:(r + 1) * rows, :]
        return (jnp.dot(a, wg, preferred_element_type=F32),
                jnp.dot(a, wu, preferred_element_type=F32))

    nxt = dots(0)
    for r in range(MM_CHUNKS):
        g, u = nxt
        if r + 1 < MM_CHUNKS:
            nxt = dots(r + 1)
        sl = slice(r * rows, (r + 1) * rows)
        rstd = jnp.tile(rstd_ref[sl, :], (1, o_ref.shape[1] // LANES))
        g = g * rstd
        u = u * rstd
        o_ref[sl, :] = (g * (1.0 / (1.0 + jnp.exp(-g))) * u).astype(o_ref.dtype)


def _mm_res_kernel(a_ref, w_ref, r_ref, gain_ref, o_ref, ob_ref, rstd_ref, *, width):
    x = r_ref[...] + jnp.dot(a_ref[...], w_ref[...], preferred_element_type=F32)
    o_ref[...] = x
    ob_ref[...] = (x * gain_ref[...]).astype(ob_ref.dtype)
    part = jnp.broadcast_to(jnp.sum(x * x, axis=-1, keepdims=True), rstd_ref.shape)
    j = pl.program_id(1)

    @pl.when(j == 0)
    def _():
        rstd_ref[...] = part

    @pl.when(j > 0)
    def _():
        rstd_ref[...] += part

    @pl.when(j == pl.num_programs(1) - 1)
    def _():
        rstd_ref[...] = lax.rsqrt(rstd_ref[...] * (1.0 / width) + EPS)


def _xprep_kernel(x_ref, gain_ref, xb_ref, rstd_ref):
    x = x_ref[...]
    xb_ref[...] = (x * gain_ref[...]).astype(xb_ref.dtype)
    ms = jnp.mean(x * x, axis=-1, keepdims=True)
    rstd_ref[...] = jnp.broadcast_to(lax.rsqrt(ms + EPS), rstd_ref.shape)


def _xprep(x, gain):
    m, d = x.shape
    tm = _pick(m, (256, 128, 64, 32, 16, 8))
    return pl.pallas_call(
        _xprep_kernel,
        grid=(m // tm,),
        in_specs=[pl.BlockSpec((tm, d), lambda i: (i, 0)),
                  pl.BlockSpec((1, d), lambda i: (0, 0))],
        out_specs=[pl.BlockSpec((tm, d), lambda i: (i, 0)),
                   pl.BlockSpec((tm, LANES), lambda i: (i, 0))],
        out_shape=[jax.ShapeDtypeStruct((m, d), BF16),
                   jax.ShapeDtypeStruct((m, LANES), F32)],
        compiler_params=_params("parallel"),
        name="xprep",
    )(x, gain)


def _row_tile(m, tm):
    return _pick(m, (tm, 512, 256, 128, 64, 32, 16, 8))


def _matmul(a, w, layer, out_dtype, tm, tn, name):
    m, k = a.shape
    n = w.shape[-1]
    tm = _row_tile(m, tm)
    tn = _pick(n, (tn, 512, 256, 128))
    return pl.pallas_call(
        _mm_kernel,
        grid=(m // tm, n // tn),
        in_specs=[pl.BlockSpec((tm, k), lambda i, j: (i, 0)),
                  pl.BlockSpec((None, k, tn), lambda i, j: (layer, 0, j))],
        out_specs=pl.BlockSpec((tm, tn), lambda i, j: (i, j)),
        out_shape=jax.ShapeDtypeStruct((m, n), out_dtype),
        compiler_params=_params("parallel", "arbitrary"),
        name=name,
    )(a, w)


def _matmul_norm(a, rstd, w_t, layer, out_dtype, tm, tn, name):
    m, k = a.shape
    n = w_t.shape[1]
    tm = _row_tile(m, tm)
    tn = _pick(n, (tn, 512, 256, 128))
    return pl.pallas_call(
        functools.partial(_mm_norm_kernel, trans_w=True),
        grid=(m // tm, n // tn),
        in_specs=[pl.BlockSpec((tm, k), lambda i, j: (i, 0)),
                  pl.BlockSpec((tm, LANES), lambda i, j: (i, 0)),
                  pl.BlockSpec((None, tn, k), lambda i, j: (layer, j, 0))],
        out_specs=pl.BlockSpec((tm, tn), lambda i, j: (i, j)),
        out_shape=jax.ShapeDtypeStruct((m, n), out_dtype),
        compiler_params=_params("parallel", "arbitrary"),
        name=name,
    )(a, rstd, w_t)


def _matmul_res(a, w, layer, residual, gain, tm, tn, name):
    m, k = a.shape
    n = w.shape[-1]
    tm = _row_tile(m, tm)
    tn = _pick(n, (tn, 512, 256, 128))
    tile = pl.BlockSpec((tm, tn), lambda i, j: (i, j))
    return pl.pallas_call(
        functools.partial(_mm_res_kernel, width=n),
        grid=(m // tm, n // tn),
        in_specs=[pl.BlockSpec((tm, k), lambda i, j: (i, 0)),
                  pl.BlockSpec((None, k, tn), lambda i, j: (layer, 0, j)),
                  tile,
                  pl.BlockSpec((1, tn), lambda i, j: (0, j))],
        out_specs=[tile, tile, pl.BlockSpec((tm, LANES), lambda i, j: (i, 0))],
        out_shape=[jax.ShapeDtypeStruct((m, n), F32),
                   jax.ShapeDtypeStruct((m, n), BF16),
                   jax.ShapeDtypeStruct((m, LANES), F32)],
        compiler_params=_params("parallel", "arbitrary"),
        name=name,
    )(a, w, residual, gain)


def _swiglu_up(a, rstd, wg, wu, layer, tm, tn):
    m, k = a.shape
    n = wg.shape[-1]
    tm = _row_tile(m, tm)
    tn = _pick(n, (tn, 256, 128))
    w_spec = pl.BlockSpec((None, k, tn), lambda i, j: (layer, 0, j))
    return pl.pallas_call(
        _mm_swiglu_kernel,
        grid=(m // tm, n // tn),
        in_specs=[pl.BlockSpec((tm, k), lambda i, j: (i, 0)),
                  pl.BlockSpec((tm, LANES), lambda i, j: (i, 0)), w_spec, w_spec],
        out_specs=pl.BlockSpec((tm, tn), lambda i, j: (i, j)),
        out_shape=jax.ShapeDtypeStruct((m, n), BF16),
        compiler_params=_params("parallel", "arbitrary"),
        name="swiglu_up",
    )(a, rstd, wg, wu)


def _split3(x):
    hi = x.astype(BF16)
    r1 = x - hi.astype(F32)
    mid = r1.astype(BF16)
    lo = (r1 - mid.astype(F32)).astype(BF16)
    return hi, mid, lo


def _tri_cat(n):
    tri = np.tril(np.ones((n, n), np.float32))
    return jnp.asarray(np.concatenate([tri, tri, tri], axis=1), dtype=BF16)


def _cumsum_rows(x, lcat):
    hi, mid, lo = _split3(x)
    stacked = jnp.concatenate([hi, mid, lo], axis=0)
    return jnp.dot(lcat, stacked, preferred_element_type=F32)


def _hgrn2_diag(qi, ki, bi, base, colid, rowid):
    a = jnp.zeros(colid.shape, F32)
    for s in range(HG_SUB):
        d = jnp.exp2(bi - bi[s:s + 1, :])
        col = jnp.sum(d * (qi * ki[s:s + 1, :]), axis=-1, keepdims=True)
        a = jnp.where(colid == base + s, col, a)
    return jnp.where(colid <= rowid + base, a, 0.0)


def _hgrn2_kernel(q_ref, f_ref, i_ref, lb_ref, lcat_ref, o_ref, st_ref, *,
                  nchunks, k_scale, heads_per_step):
    @pl.when(pl.program_id(2) == 0)
    def _():
        st_ref[...] = jnp.zeros_like(st_ref)

    nsub = HG_CHUNK // HG_SUB
    colid = lax.broadcasted_iota(jnp.int32, (HG_SUB, HG_CHUNK), 1)
    rowid = lax.broadcasted_iota(jnp.int32, (HG_SUB, HG_CHUNK), 0)
    zero_blk = jnp.zeros((HG_SUB, HEAD_DIM), F32)

    def head_chunk(g, r0):
        cols = slice(g * HEAD_DIM, (g + 1) * HEAD_DIM)
        lb = lb_ref[:, cols]
        one_m_lb = 1.0 - lb
        qp = q_ref[pl.ds(r0, HG_CHUNK), cols].astype(F32)
        z = f_ref[pl.ds(r0, HG_CHUNK), cols].astype(F32)
        v = i_ref[pl.ds(r0, HG_CHUNK), cols]
        q = qp * (1.0 / (1.0 + jnp.exp(-qp))) * k_scale
        t = jnp.exp(-jnp.abs(z))
        inv = 1.0 / (1.0 + t)
        pos = z >= 0.0
        sig = jnp.where(pos, inv, t * inv)
        nsig = jnp.where(pos, t * inv, inv)
        kk = one_m_lb * nsig
        lf = jnp.log(lb + one_m_lb * sig)
        b = _cumsum_rows(lf, lcat_ref[...]) * LOG2E
        b_last = b[HG_CHUNK - 1:HG_CHUNK, :]

        st = st_ref[g]
        qd = (q * jnp.exp2(b)).astype(BF16)
        o = lax.dot_general(qd, st.astype(BF16), (((1,), (1,)), ((), ())),
                            preferred_element_type=F32)

        rows = []
        kparts = []
        r_prev = None
        for i in range(nsub):
            lo_r = i * HG_SUB
            bi = b[lo_r:lo_r + HG_SUB, :]
            qi = q[lo_r:lo_r + HG_SUB, :]
            ki = kk[lo_r:lo_r + HG_SUB, :]
            a_i = _hgrn2_diag(qi, ki, bi, lo_r, colid, rowid)
            if i > 0:
                r = b[lo_r - 1:lo_r, :]
                if kparts:
                    fac = jnp.exp2(r - r_prev)
                    kparts = [kp * fac for kp in kparts]
                b_prev = b[lo_r - HG_SUB:lo_r, :]
                kparts.append(kk[lo_r - HG_SUB:lo_r, :] * jnp.exp2(r - b_prev))
                r_prev = r
                kr = jnp.concatenate(kparts + [zero_blk] * (nsub - i), axis=0)
                qr = (qi * jnp.exp2(bi - r)).astype(BF16)
                a_i = a_i + lax.dot_general(qr, kr.astype(BF16),
                                            (((1,), (1,)), ((), ())),
                                            preferred_element_type=F32)
            rows.append(a_i)
        a = jnp.concatenate(rows, axis=0).astype(BF16)
        o = o + jnp.dot(a, v, preferred_element_type=F32)
        o_ref[pl.ds(r0, HG_CHUNK), cols] = o.astype(o_ref.dtype)

        fac = jnp.exp2(b_last - r_prev)
        b_tail = b[HG_CHUNK - HG_SUB:, :]
        kparts = [kp * fac for kp in kparts]
        kparts.append(kk[HG_CHUNK - HG_SUB:, :] * jnp.exp2(b_last - b_tail))
        kd = jnp.concatenate(kparts, axis=0).astype(BF16)
        upd = lax.dot_general(v, kd, (((0,), (0,)), ((), ())),
                              preferred_element_type=F32)
        st_ref[g] = st * jnp.exp2(b_last) + upd

    def chunk(c, carry):
        r0 = pl.multiple_of(c * HG_CHUNK, HG_CHUNK)
        for g in range(heads_per_step):
            head_chunk(g, r0)
        return carry

    lax.fori_loop(0, nchunks, chunk, 0)


def _hgrn2(proj, lb_all, layer, batch, seq, heads, lcat):
    m = proj.shape[0]
    t = _pick(seq, (512, 256, 128))
    nt = seq // t

    hps = _pick(heads, (HG_HEADS_PER_STEP, 1))
    width = hps * HEAD_DIM
    groups = heads // hps

    def col(off):
        return pl.BlockSpec((t, width), lambda b, h, s: (b * nt + s, off + h))

    kern = functools.partial(_hgrn2_kernel, nchunks=t // HG_CHUNK,
                             k_scale=float(HEAD_DIM) ** -0.5, heads_per_step=hps)
    return pl.pallas_call(
        kern,
        grid=(batch, groups, nt),
        in_specs=[col(0), col(groups), col(2 * groups),
                  pl.BlockSpec((None, 1, width), lambda b, h, s: (layer, 0, h)),
                  pl.BlockSpec(lcat.shape, lambda b, h, s: (0, 0))],
        out_specs=pl.BlockSpec((t, width), lambda b, h, s: (b * nt + s, h)),
        out_shape=jax.ShapeDtypeStruct((m, heads * HEAD_DIM), BF16),
        scratch_shapes=[pltpu.VMEM((hps, HEAD_DIM, HEAD_DIM), F32)],
        compiler_params=_params("parallel", "parallel", "arbitrary"),
        name="hgrn2",
    )(proj, proj, proj, lb_all, lcat)


def _fox_prep_kernel(ff_ref, bias_ref, fq_ref, fk_ref, lcat_ref, pq_ref, pk_ref,
                     cq_ref, ck_ref, qcat_ref, kcat_ref, carry_ref, *, heads):
    @pl.when(pl.program_id(1) == 0)
    def _():
        carry_ref[...] = jnp.zeros_like(carry_ref)

    x = ff_ref[...] + bias_ref[...]
    lf = jnp.minimum(x, 0.0) - jnp.log(1.0 + jnp.exp(-jnp.abs(x)))
    c = _cumsum_rows(lf, lcat_ref[...]) + carry_ref[...]
    carry_ref[...] = c[c.shape[0] - 1:, :]
    e = jnp.concatenate(_split3(c * LOG2E), axis=1)
    qe = jnp.dot(e, pq_ref[...], preferred_element_type=F32) + cq_ref[...]
    ke = jnp.dot(e, pk_ref[...], preferred_element_type=F32) + ck_ref[...]
    for h in range(heads):
        src = slice(h * HEAD_DIM, (h + 1) * HEAD_DIM)
        lo = 2 * h * HEAD_DIM
        qcat_ref[:, lo:lo + HEAD_DIM] = fq_ref[:, src]
        qcat_ref[:, lo + HEAD_DIM:lo + 2 * HEAD_DIM] = qe[:, src].astype(qcat_ref.dtype)
        kcat_ref[:, lo:lo + HEAD_DIM] = fk_ref[:, src]
        kcat_ref[:, lo + HEAD_DIM:lo + 2 * HEAD_DIM] = ke[:, src].astype(kcat_ref.dtype)


def _fox_expanders(heads):
    width = heads * HEAD_DIM
    pq = np.zeros((3 * LANES, width), np.float32)
    pk = np.zeros((3 * LANES, width), np.float32)
    cq = np.zeros((1, width), np.float32)
    ck = np.zeros((1, width), np.float32)
    for h in range(heads):
        for part in range(3):
            pq[part * LANES + h, h * HEAD_DIM + part] = 1.0
            pk[part * LANES + h, h * HEAD_DIM + 3 + part] = -1.0
            cq[0, h * HEAD_DIM + 3 + part] = 1.0
            ck[0, h * HEAD_DIM + part] = 1.0
    return (jnp.asarray(pq, BF16), jnp.asarray(pk, BF16),
            jnp.asarray(cq, F32), jnp.asarray(ck, F32))


def _fox_prep(ff, bias_row, proj, fq_slab, fk_slab, batch, seq, heads):
    m = ff.shape[0]
    blk = _pick(seq, (256, 128))
    nb = seq // blk
    width = heads * HEAD_DIM
    lcat = _tri_cat(blk)
    pq, pk, cq, ck = _fox_expanders(heads)
    full = lambda a: pl.BlockSpec(a.shape, lambda b, s: (0, 0))
    slab = lambda cb: pl.BlockSpec((blk, width), lambda b, s: (b * nb + s, cb))
    out_spec = pl.BlockSpec((blk, 2 * width), lambda b, s: (b * nb + s, 0))
    return pl.pallas_call(
        functools.partial(_fox_prep_kernel, heads=heads),
        grid=(batch, nb),
        in_specs=[pl.BlockSpec((blk, LANES), lambda b, s: (b * nb + s, 0)),
                  full(bias_row), slab(fq_slab), slab(fk_slab),
                  full(lcat), full(pq), full(pk), full(cq), full(ck)],
        out_specs=[out_spec, out_spec],
        out_shape=[jax.ShapeDtypeStruct((m, 2 * width), BF16)] * 2,
        scratch_shapes=[pltpu.VMEM((1, LANES), F32)],
        compiler_params=_params("parallel", "arbitrary"),
        name="fox_prep",
    )(ff, bias_row, proj, proj, lcat, pq, pk, cq, ck)


def _rope(t, cos_t, sin_t):
    half = MLA_ROPE // 2
    lane = lax.broadcasted_iota(jnp.int32, t.shape, 1)
    swapped = jnp.where(lane < half,
                        pltpu.roll(t, LANES - half, 1),
                        pltpu.roll(t, half, 1))
    return t * cos_t + swapped * sin_t


def _flash_kernel(qi_ref, kj_ref, *refs, rope_q, ratio):
    if rope_q:
        (q_ref, cos_ref, sin_ref, k_ref, v_ref, o_ref,
         qf_ref, va_ref, m_ref, acc_ref) = refs
    else:
        q_ref, k_ref, v_ref, o_ref, va_ref, m_ref, acc_ref = refs
    tq = q_ref.shape[0]
    tk = k_ref.shape[0]
    p_idx = pl.program_id(2)
    qi = qi_ref[p_idx]
    kj = kj_ref[p_idx]

    @pl.when(kj == 0)
    def _():
        if rope_q:
            qf_ref[:, :HEAD_DIM] = q_ref[:, :HEAD_DIM]
            q2 = _rope(q_ref[:, HEAD_DIM:].astype(F32), cos_ref[...], sin_ref[...])
            qf_ref[:, HEAD_DIM:] = q2.astype(qf_ref.dtype)
        va_ref[:, HEAD_DIM:] = jnp.ones((tk, HEAD_DIM), va_ref.dtype)
        m_ref[...] = jnp.full_like(m_ref, MASK_VALUE)
        acc_ref[...] = jnp.zeros_like(acc_ref)

    def step(masked):
        qsrc = qf_ref if rope_q else q_ref
        va_ref[:, :HEAD_DIM] = v_ref[...]
        rows = tq // FLASH_CHAINS

        diag = masked and ratio == 1
        span = tk // FLASH_KV_SPLIT

        stages = []
        for r in range(FLASH_CHAINS):
            kv_len = (r + 1) * rows if diag else tk
            stages += [(r, c0, min(c0 + span, kv_len)) for c0 in range(0, kv_len, span)]

        def logits(stage):
            r, c0, c1 = stage
            return lax.dot_general(qsrc[r * rows:(r + 1) * rows, :], k_ref[c0:c1, :],
                                   (((1,), (1,)), ((), ())),
                                   preferred_element_type=F32)

        s_next = logits(stages[0])
        for idx, (r, c0, c1) in enumerate(stages):
            s = s_next
            if idx + 1 < len(stages):
                s_next = logits(stages[idx + 1])
            sl = slice(r * rows, (r + 1) * rows)
            if diag and c1 > r * rows:
                d0 = r * rows - c0
                tri = (lax.broadcasted_iota(jnp.int32, (rows, rows), 1)
                       <= lax.broadcasted_iota(jnp.int32, (rows, rows), 0))
                s_diag = jnp.where(tri, s[:, d0:], MASK_VALUE)
                s = s_diag if d0 == 0 else jnp.concatenate([s[:, :d0], s_diag], axis=1)
            elif masked and not diag:
                row = lax.broadcasted_iota(jnp.int32, s.shape, 0) + (qi * tq + r * rows)
                col = lax.broadcasted_iota(jnp.int32, s.shape, 1) + (kj * tk + c0)
                s = jnp.where(col <= row, s, MASK_VALUE)
            m_prev = m_ref[sl, :]
            m_new = jnp.maximum(m_prev, jnp.max(s, axis=-1, keepdims=True))
            alpha = jnp.exp2(m_prev - m_new)
            p = jnp.exp2(s - jnp.tile(m_new, (1, s.shape[1] // LANES)))
            acc_ref[sl, :] = jnp.tile(alpha, (1, 2)) * acc_ref[sl, :] + jnp.dot(
                p.astype(va_ref.dtype), va_ref[c0:c1, :], preferred_element_type=F32)
            m_ref[sl, :] = m_new

    @pl.when(kj < qi * ratio)
    def _():
        step(False)

    @pl.when(kj >= qi * ratio)
    def _():
        step(True)

    @pl.when(kj == (qi + 1) * ratio - 1)
    def _():
        acc = acc_ref[...]
        o_ref[...] = (acc[:, :HEAD_DIM] / acc[:, HEAD_DIM:]).astype(o_ref.dtype)


def _flash(q, k, k_off, v, v_off, batch, seq, heads, rope=None):
    m = q.shape[0]
    tq = _pick(seq, (FLASH_TQ, 512, 256, 128))
    tk = _pick(tq, (FLASH_TK, 256, 128))
    nq = seq // tq
    nk = seq // tk
    ratio = tq // tk
    pairs = [(i, j) for i in range(nq) for j in range((i + 1) * ratio)]
    qi_tab = jnp.asarray([p[0] for p in pairs], jnp.int32)
    kj_tab = jnp.asarray([p[1] for p in pairs], jnp.int32)

    in_specs = [pl.BlockSpec((tq, 2 * HEAD_DIM),
                             lambda b, h, p, qi, kj: (b * nq + qi[p], h))]
    args = [q]
    scratch = []
    if rope is not None:
        tab = pl.BlockSpec((tq, LANES), lambda b, h, p, qi, kj: (b * nq + qi[p], 0))
        in_specs += [tab, tab]
        args += list(rope)
        scratch.append(pltpu.VMEM((tq, 2 * HEAD_DIM), BF16))
    in_specs += [pl.BlockSpec((tk, 2 * HEAD_DIM),
                              lambda b, h, p, qi, kj: (b * nk + kj[p], k_off + h)),
                 pl.BlockSpec((tk, HEAD_DIM),
                              lambda b, h, p, qi, kj: (b * nk + kj[p], v_off + h))]
    args += [k, v]
    scratch += [pltpu.VMEM((tk, 2 * HEAD_DIM), BF16),
                pltpu.VMEM((tq, LANES), F32),
                pltpu.VMEM((tq, 2 * HEAD_DIM), F32)]
    return pl.pallas_call(
        functools.partial(_flash_kernel, rope_q=rope is not None, ratio=ratio),
        grid_spec=pltpu.PrefetchScalarGridSpec(
            num_scalar_prefetch=2,
            grid=(batch, heads, len(pairs)),
            in_specs=in_specs,
            out_specs=pl.BlockSpec((tq, HEAD_DIM),
                                   lambda b, h, p, qi, kj: (b * nq + qi[p], h)),
            scratch_shapes=scratch),
        out_shape=jax.ShapeDtypeStruct((m, heads * HEAD_DIM), BF16),
        compiler_params=_params("parallel", "parallel", "arbitrary"),
        name="flash_rope" if rope is not None else "flash",
    )(qi_tab, kj_tab, *args)


def _mla_prep_kernel(mq_ref, mkv_ref, qw_ref, kvw_ref, cos_ref, sin_ref,
                     qn_ref, kvin_ref, *, kv_rank):
    mq = mq_ref[...].astype(F32)
    ms = jnp.mean(mq * mq, axis=-1, keepdims=True)
    qn_ref[...] = (mq * lax.rsqrt(ms + EPS) * qw_ref[...]).astype(qn_ref.dtype)
    lat = mkv_ref[:, :kv_rank].astype(F32)
    ms = jnp.mean(lat * lat, axis=-1, keepdims=True)
    kvin_ref[:, :kv_rank] = (lat * lax.rsqrt(ms + EPS) * kvw_ref[...]).astype(kvin_ref.dtype)
    kpe = mkv_ref[:, kv_rank:kv_rank + LANES].astype(F32)
    kvin_ref[:, kv_rank:] = _rope(kpe, cos_ref[...], sin_ref[...]).astype(kvin_ref.dtype)


def _mla_prep(proj, mq_blk, mkv_blk, q_rank, kv_rank, qw, kvw, cos_t, sin_t):
    m = proj.shape[0]
    tm = _pick(m, (256, 128, 64, 32, 16, 8))
    row = lambda width, blk: pl.BlockSpec((tm, width), lambda i: (i, blk))
    full = lambda a: pl.BlockSpec(a.shape, lambda i: (0, 0))
    return pl.pallas_call(
        functools.partial(_mla_prep_kernel, kv_rank=kv_rank),
        grid=(m // tm,),
        in_specs=[row(q_rank, mq_blk), row(q_rank, mkv_blk), full(qw), full(kvw),
                  row(LANES, 0), row(LANES, 0)],
        out_specs=[row(q_rank, 0), row(kv_rank + LANES, 0)],
        out_shape=[jax.ShapeDtypeStruct((m, q_rank), BF16),
                   jax.ShapeDtypeStruct((m, kv_rank + LANES), BF16)],
        compiler_params=_params("parallel"),
        name="mla_prep",
    )(proj, proj, qw, kvw, cos_t, sin_t)


def _postmix_kernel(ohg_ref, g_ref, ofox_ref, omla_ref, hgw_ref, fw_ref, mw_ref,
                    y_ref, *, hg_w, fox_w):
    def normed(o_ref, w_ref):
        o = o_ref[...].astype(F32)
        ms = jnp.mean(o * o, axis=-1, keepdims=True)
        return o * lax.rsqrt(ms + EPS) * w_ref[...]

    g = g_ref[...].astype(F32)
    y_hg = normed(ohg_ref, hgw_ref) * (g * (1.0 / (1.0 + jnp.exp(-g))))
    y_ref[:, :hg_w] = y_hg.astype(y_ref.dtype)
    y_ref[:, hg_w:hg_w + fox_w] = normed(ofox_ref, fw_ref).astype(y_ref.dtype)
    y_ref[:, hg_w + fox_w:] = normed(omla_ref, mw_ref).astype(y_ref.dtype)


def _postmix(o_hg, proj, g_blk, o_fox, o_mla, hgw, fw, mw):
    m, hg_w = o_hg.shape
    fox_w = o_fox.shape[1]
    mla_w = o_mla.shape[1]
    tm = _pick(m, (256, 128, 64, 32, 16, 8))
    row = lambda width, blk: pl.BlockSpec((tm, width), lambda i: (i, blk))
    full = lambda a: pl.BlockSpec(a.shape, lambda i: (0, 0))
    return pl.pallas_call(
        functools.partial(_postmix_kernel, hg_w=hg_w, fox_w=fox_w),
        grid=(m // tm,),
        in_specs=[row(hg_w, 0), row(hg_w, g_blk), row(fox_w, 0), row(mla_w, 0),
                  full(hgw), full(fw), full(mw)],
        out_specs=row(hg_w + fox_w + mla_w, 0),
        out_shape=jax.ShapeDtypeStruct((m, hg_w + fox_w + mla_w), BF16),
        compiler_params=_params("parallel"),
        name="postmix",
    )(o_hg, proj, o_fox, o_mla, hgw, fw, mw)


def kernel(x, positions, mix_norm, w_in, hg_lower_bounds, hg_out_norm, fox_f_bias,
           fox_out_norm, mla_q_a_norm, mla_w_q_b, mla_kv_a_norm, mla_w_kv_b,
           mla_out_norm, w_o, ffn_norm, w_gate, w_up, w_down, final_norm):
    batch, seq, d_model = x.shape
    depth = w_in.shape[0]
    m = batch * seq
    hg_w = hg_lower_bounds.shape[1]
    hg_heads = hg_w // HEAD_DIM
    fox_heads = fox_f_bias.shape[1]
    fox_w = fox_heads * HEAD_DIM
    q_rank = mla_q_a_norm.shape[1]
    kv_rank = mla_kv_a_norm.shape[1]
    mla_heads = mla_w_q_b.shape[2] // (HEAD_DIM + MLA_ROPE)
    mla_w = mla_heads * HEAD_DIM
    assert kv_rank + LANES <= q_rank and fox_heads <= LANES
    assert hg_w == q_rank and (6 * hg_w) % fox_w == 0

    off_fox = 4 * hg_w
    off_ff = off_fox + 3 * fox_w
    off_mq = off_ff + fox_heads
    off_mkv = off_mq + q_rank
    fox_scale = float(HEAD_DIM) ** -0.5 * LOG2E
    mla_scale = float(HEAD_DIM + MLA_ROPE) ** -0.5 * LOG2E
    pad_kv = q_rank - (kv_rank + MLA_ROPE)
    w_in_t = jnp.transpose(w_in, (0, 2, 1))
    w_proj = jnp.concatenate([
        w_in_t[:, :off_fox].astype(BF16),
        w_in_t[:, off_mq:off_mkv].astype(BF16),
        w_in_t[:, off_mkv:].astype(BF16),
        jnp.zeros((depth, pad_kv, d_model), BF16),
        (w_in_t[:, off_fox:off_fox + fox_w] * fox_scale).astype(BF16),
        w_in_t[:, off_fox + fox_w:off_ff].astype(BF16),
    ], axis=1)
    w_ff = jnp.pad(w_in_t[:, off_ff:off_mq],
                   ((0, 0), (0, LANES - fox_heads), (0, 0))).astype(BF16)
    blk_mq = 4
    blk_mkv = 5
    slab_fq = (6 * hg_w) // fox_w
    blk_fv = (6 * hg_w + 2 * fox_w) // HEAD_DIM

    wq = mla_w_q_b.reshape(depth, q_rank, mla_heads, HEAD_DIM + MLA_ROPE) * mla_scale
    w_qb = jnp.pad(wq, ((0, 0),) * 3 + ((0, HEAD_DIM - MLA_ROPE),)).reshape(
        depth, q_rank, 2 * mla_w).astype(BF16)
    wkv = mla_w_kv_b.reshape(depth, kv_rank, mla_heads, 2 * HEAD_DIM)
    k_cols = jnp.pad(wkv[..., :HEAD_DIM], ((0, 0),) * 3 + ((0, HEAD_DIM),)).reshape(
        depth, kv_rank, 2 * mla_w)
    v_cols = wkv[..., HEAD_DIM:].reshape(depth, kv_rank, mla_w)
    route = np.zeros((LANES, mla_heads, 2 * HEAD_DIM), np.float32)
    route[:, :, HEAD_DIM:] = np.eye(LANES, dtype=np.float32)[:, None, :]
    route = np.concatenate([route.reshape(LANES, 2 * mla_w),
                            np.zeros((LANES, mla_w), np.float32)], axis=1)
    w_kvb = jnp.concatenate([
        jnp.concatenate([k_cols, v_cols], axis=2),
        jnp.broadcast_to(jnp.asarray(route), (depth, LANES, 3 * mla_w)),
    ], axis=1).astype(BF16)
    w_o_b = w_o.astype(BF16)
    w_down_b = w_down.astype(BF16)

    lb_soft = jax.nn.softmax(hg_lower_bounds.astype(F32), axis=0)
    lb_all = (jnp.cumsum(lb_soft, axis=0) - lb_soft[0]).reshape(depth, 1, hg_w)
    fox_bias = jnp.pad(fox_f_bias.astype(F32),
                       ((0, 0), (0, LANES - fox_heads))).reshape(depth, 1, LANES)

    inv_freq = ROPE_BASE ** (-jnp.arange(0, MLA_ROPE, 2, dtype=F32) / MLA_ROPE)
    angles = positions.astype(F32).reshape(m, 1) * inv_freq
    cos, sin = jnp.cos(angles), jnp.sin(angles)
    zpad = jnp.zeros((m, LANES - MLA_ROPE), F32)
    cos_t = jnp.concatenate([cos, cos, zpad], axis=1)
    sin_t = jnp.concatenate([-sin, sin, zpad], axis=1)

    hg_lcat = _tri_cat(HG_CHUNK)
    xr = x.reshape(m, d_model).astype(F32)
    mix_gain = mix_norm.astype(F32).reshape(depth, 1, d_model)
    ffn_gain = ffn_norm.astype(F32).reshape(depth, 1, d_model)
    xb, rstd = _xprep(xr, mix_gain[0])

    for l in range(depth):
        proj = _matmul_norm(xb, rstd, w_proj, l, BF16, 2048, 512, name="in_proj")
        ff = _matmul_norm(xb, rstd, w_ff, l, F32, 1024, LANES, name="ff_proj")

        o_hg = _hgrn2(proj, lb_all, l, batch, seq, hg_heads, hg_lcat)

        qcat, kcat = _fox_prep(ff, fox_bias[l], proj, slab_fq, slab_fq + 1,
                               batch, seq, fox_heads)
        o_fox = _flash(qcat, kcat, 0, proj, blk_fv, batch, seq, fox_heads)

        qn, kvin = _mla_prep(proj, blk_mq, blk_mkv, q_rank, kv_rank,
                             mla_q_a_norm[l].reshape(1, q_rank).astype(F32),
                             mla_kv_a_norm[l].reshape(1, kv_rank).astype(F32),
                             cos_t, sin_t)
        q_mla = _matmul(qn, w_qb, l, BF16, 2048, mla_w, name="mla_q_b")
        kv_mla = _matmul(kvin, w_kvb, l, BF16, 2048, mla_w, name="mla_kv_b")
        o_mla = _flash(q_mla, kv_mla, 0, kv_mla, 2 * mla_heads, batch, seq,
                       mla_heads, rope=(cos_t, sin_t))

        y = _postmix(o_hg, proj, 3, o_fox, o_mla,
                     hg_out_norm[l].reshape(1, hg_w).astype(F32),
                     fox_out_norm[l].reshape(1, fox_w).astype(F32),
                     mla_out_norm[l].reshape(1, mla_w).astype(F32))
        xr, xb, rstd = _matmul_res(y, w_o_b, l, xr, ffn_gain[l], 1024, 512,
                                   name="out_proj")

        act = _swiglu_up(xb, rstd, w_gate, w_up, l, 1024, 256)
        xr, xb, rstd = _matmul_res(act, w_down_b, l, xr, mix_gain[(l + 1) % depth],
                                   512, 512, name="down_proj")

    out = _rmsnorm(xr, final_norm, x.dtype)
    return out.reshape(batch, seq, d_model)
```

```python
import functools
import math

import numpy as np
import jax
import jax.numpy as jnp
from jax import lax
from jax.experimental import pallas as pl
from jax.experimental.pallas import tpu as pltpu

F32 = jnp.float32
BF16 = jnp.bfloat16

LANES = 128
HEAD_DIM = 128
MLA_ROPE = 64
ROPE_BASE = 10000.0
EPS = 1e-6
MASK_VALUE = -1e30
LOG2E = math.log2(math.e)
HG_CHUNK = 128
HG_SUB = 16
HG_HEADS_PER_STEP = 4
FLASH_TQ = 2048
FLASH_TK = 2048
MM_CHUNKS = 2
FLASH_CHAINS = 4
FLASH_KV_SPLIT = 1
VMEM_LIMIT_BYTES = 52 * 1024 * 1024


def _params(*semantics):
    return pltpu.CompilerParams(dimension_semantics=semantics,
                                vmem_limit_bytes=VMEM_LIMIT_BYTES)


def _pick(n, candidates):
    for c in candidates:
        if n % c == 0:
            return c
    return n


def _rmsnorm_kernel(x_ref, w_ref, o_ref):
    x = x_ref[...].astype(F32)
    ms = jnp.mean(x * x, axis=-1, keepdims=True)
    o_ref[...] = (x * lax.rsqrt(ms + EPS) * w_ref[...]).astype(o_ref.dtype)


def _rmsnorm(x, w, out_dtype):
    m, d = x.shape
    tm = _pick(m, (256, 128, 64, 32, 16, 8))
    return pl.pallas_call(
        _rmsnorm_kernel,
        grid=(m // tm,),
        in_specs=[pl.BlockSpec((tm, d), lambda i: (i, 0)),
                  pl.BlockSpec((1, d), lambda i: (0, 0))],
        out_specs=pl.BlockSpec((tm, d), lambda i: (i, 0)),
        out_shape=jax.ShapeDtypeStruct((m, d), out_dtype),
        compiler_params=_params("parallel"),
        name="rmsnorm",
    )(x, w.reshape(1, d).astype(F32))


def _rstd_tile(rstd_ref, width):
    return jnp.tile(rstd_ref[...], (1, width // LANES))


def _dot(a, w, trans_w):
    if trans_w:
        return lax.dot_general(a, w, (((1,), (1,)), ((), ())),
                               preferred_element_type=F32)
    return jnp.dot(a, w, preferred_element_type=F32)


def _mm_kernel(a_ref, w_ref, o_ref):
    o_ref[...] = _dot(a_ref[...], w_ref[...], False).astype(o_ref.dtype)


def _mm_norm_kernel(a_ref, rstd_ref, w_ref, o_ref, *, trans_w):
    acc = _dot(a_ref[...], w_ref[...], trans_w)
    o_ref[...] = (acc * _rstd_tile(rstd_ref, acc.shape[1])).astype(o_ref.dtype)


def _mm_swiglu_kernel(a_ref, rstd_ref, wg_ref, wu_ref, wd_ref, o_ref, wdb_ref):
    wdb_ref[...] = wd_ref[...].astype(wdb_ref.dtype)
    rows = a_ref.shape[0] // MM_CHUNKS
    wg = wg_ref[...].astype(a_ref.dtype)
    wu = wu_ref[...].astype(a_ref.dtype)

    def dots(r):
        a = a_ref[r * rows:(r + 1) * rows, :]
        return (jnp.dot(a, wg, preferred_element_type=F32),
                jnp.dot(a, wu, preferred_element_type=F32))

    nxt = dots(0)
    for r in range(MM_CHUNKS):
        g, u = nxt
        if r + 1 < MM_CHUNKS:
            nxt = dots(r + 1)
        sl = slice(r * rows, (r + 1) * rows)
        rstd = jnp.tile(rstd_ref[sl, :], (1, o_ref.shape[1] // LANES))
        g = g * rstd
        u = u * rstd
        o_ref[sl, :] = (g * (1.0 / (1.0 + jnp.exp(-g))) * u).astype(o_ref.dtype)


def _mm_res_kernel(a_ref, w_ref, r_ref, gain_ref, o_ref, ob_ref, rstd_ref, *, width):
    x = r_ref[...] + jnp.dot(a_ref[...], w_ref[...], preferred_element_type=F32)
    o_ref[...] = x
    ob_ref[...] = (x * gain_ref[...]).astype(ob_ref.dtype)
    part = jnp.broadcast_to(jnp.sum(x * x, axis=-1, keepdims=True), rstd_ref.shape)
    j = pl.program_id(1)

    @pl.when(j == 0)
    def _():
        rstd_ref[...] = part

    @pl.when(j > 0)
    def _():
        rstd_ref[...] += part

    @pl.when(j == pl.num_programs(1) - 1)
    def _():
        rstd_ref[...] = lax.rsqrt(rstd_ref[...] * (1.0 / width) + EPS)


def _xprep_kernel(x_ref, gain_ref, xb_ref, rstd_ref):
    x = x_ref[...]
    xb_ref[...] = (x * gain_ref[...]).astype(xb_ref.dtype)
    ms = jnp.mean(x * x, axis=-1, keepdims=True)
    rstd_ref[...] = jnp.broadcast_to(lax.rsqrt(ms + EPS), rstd_ref.shape)


def _xprep(x, gain):
    m, d = x.shape
    tm = _pick(m, (256, 128, 64, 32, 16, 8))
    return pl.pallas_call(
        _xprep_kernel,
        grid=(m // tm,),
        in_specs=[pl.BlockSpec((tm, d), lambda i: (i, 0)),
                  pl.BlockSpec((1, d), lambda i: (0, 0))],
        out_specs=[pl.BlockSpec((tm, d), lambda i: (i, 0)),
                   pl.BlockSpec((tm, LANES), lambda i: (i, 0))],
        out_shape=[jax.ShapeDtypeStruct((m, d), BF16),
                   jax.ShapeDtypeStruct((m, LANES), F32)],
        compiler_params=_params("parallel"),
        name="xprep",
    )(x, gain)


def _row_tile(m, tm):
    return _pick(m, (tm, 512, 256, 128, 64, 32, 16, 8))


def _matmul(a, w, layer, out_dtype, tm, tn, name):
    m, k = a.shape
    n = w.shape[-1]
    tm = _row_tile(m, tm)
    tn = _pick(n, (tn, 512, 256, 128))
    return pl.pallas_call(
        _mm_kernel,
        grid=(m // tm, n // tn),
        in_specs=[pl.BlockSpec((tm, k), lambda i, j: (i, 0)),
                  pl.BlockSpec((None, k, tn), lambda i, j: (layer, 0, j))],
        out_specs=pl.BlockSpec((tm, tn), lambda i, j: (i, j)),
        out_shape=jax.ShapeDtypeStruct((m, n), out_dtype),
        compiler_params=_params("parallel", "arbitrary"),
        name=name,
    )(a, w)


def _matmul_norm(a, rstd, w_t, layer, out_dtype, tm, tn, name):
    m, k = a.shape
    n = w_t.shape[1]
    tm = _row_tile(m, tm)
    tn = _pick(n, (tn, 512, 256, 128))
    return pl.pallas_call(
        functools.partial(_mm_norm_kernel, trans_w=True),
        grid=(m // tm, n // tn),
        in_specs=[pl.BlockSpec((tm, k), lambda i, j: (i, 0)),
                  pl.BlockSpec((tm, LANES), lambda i, j: (i, 0)),
                  pl.BlockSpec((None, tn, k), lambda i, j: (layer, j, 0))],
        out_specs=pl.BlockSpec((tm, tn), lambda i, j: (i, j)),
        out_shape=jax.ShapeDtypeStruct((m, n), out_dtype),
        compiler_params=_params("parallel", "arbitrary"),
        name=name,
    )(a, rstd, w_t)


def _matmul_res(a, w, layer, residual, gain, tm, tn, name):
    m, k = a.shape
    n = w.shape[-1]
    tm = _row_tile(m, tm)
    tn = _pick(n, (tn, 512, 256, 128))
    tile = pl.BlockSpec((tm, tn), lambda i, j: (i, j))
    return pl.pallas_call(
        functools.partial(_mm_res_kernel, width=n),
        grid=(m // tm, n // tn),
        in_specs=[pl.BlockSpec((tm, k), lambda i, j: (i, 0)),
                  pl.BlockSpec((None, k, tn), lambda i, j: (layer, 0, j)),
                  tile,
                  pl.BlockSpec((1, tn), lambda i, j: (0, j))],
        out_specs=[tile, tile, pl.BlockSpec((tm, LANES), lambda i, j: (i, 0))],
        out_shape=[jax.ShapeDtypeStruct((m, n), F32),
                   jax.ShapeDtypeStruct((m, n), BF16),
                   jax.ShapeDtypeStruct((m, LANES), F32)],
        compiler_params=_params("parallel", "arbitrary"),
        name=name,
    )(a, w, residual, gain)


def _swiglu_up(a, rstd, wg, wu, wd, layer, tm, tn):
    m, k = a.shape
    n = wg.shape[-1]
    tm = _row_tile(m, tm)
    tn = _pick(n, (tn, 256, 128))
    ni, nj = m // tm, n // tn
    slab = wd.shape[1] // (ni * nj)
    assert slab * ni * nj == wd.shape[1] and slab % 16 == 0
    w_spec = pl.BlockSpec((None, k, tn), lambda i, j: (layer, 0, j))
    return pl.pallas_call(
        _mm_swiglu_kernel,
        grid=(ni, nj),
        in_specs=[pl.BlockSpec((tm, k), lambda i, j: (i, 0)),
                  pl.BlockSpec((tm, LANES), lambda i, j: (i, 0)), w_spec, w_spec,
                  pl.BlockSpec((None, slab, wd.shape[2]),
                               lambda i, j: (layer, i * nj + j, 0))],
        out_specs=[pl.BlockSpec((tm, tn), lambda i, j: (i, j)),
                   pl.BlockSpec((None, slab, wd.shape[2]), lambda i, j: (0, i * nj + j, 0))],
        out_shape=[jax.ShapeDtypeStruct((m, n), BF16),
                   jax.ShapeDtypeStruct((1,) + wd.shape[1:], BF16)],
        compiler_params=_params("parallel", "arbitrary"),
        name="swiglu_up",
    )(a, rstd, wg, wu, wd)


def _split3(x):
    hi = x.astype(BF16)
    r1 = x - hi.astype(F32)
    mid = r1.astype(BF16)
    lo = (r1 - mid.astype(F32)).astype(BF16)
    return hi, mid, lo


def _tri_cat(n):
    tri = np.tril(np.ones((n, n), np.float32))
    return jnp.asarray(np.concatenate([tri, tri, tri], axis=1), dtype=BF16)


def _cumsum_rows(x, lcat):
    hi, mid, lo = _split3(x)
    stacked = jnp.concatenate([hi, mid, lo], axis=0)
    return jnp.dot(lcat, stacked, preferred_element_type=F32)


def _hgrn2_diag(qi, ki, bi, base, colid, rowid):
    half = HG_SUB // 2
    halves = []
    for t0 in (0, half):
        bt = bi[t0:t0 + half, :]
        qt = qi[t0:t0 + half, :]
        cid = colid[:half, :]
        a = jnp.zeros(cid.shape, F32)
        for s in range(HG_SUB if t0 else half):
            d = jnp.exp2(bt - bi[s:s + 1, :])
            col = jnp.sum(d * (qt * ki[s:s + 1, :]), axis=-1, keepdims=True)
            a = jnp.where(cid == base + s, col, a)
        halves.append(a)
    a = jnp.concatenate(halves, axis=0)
    return jnp.where(colid <= rowid + base, a, 0.0)


def _hgrn2_kernel(q_ref, f_ref, i_ref, lb_ref, lcat_ref, o_ref, st_ref, *,
                  nchunks, k_scale, heads_per_step):
    @pl.when(pl.program_id(2) == 0)
    def _():
        st_ref[...] = jnp.zeros_like(st_ref)

    nsub = HG_CHUNK // HG_SUB
    colid = lax.broadcasted_iota(jnp.int32, (HG_SUB, HG_CHUNK), 1)
    rowid = lax.broadcasted_iota(jnp.int32, (HG_SUB, HG_CHUNK), 0)
    zero_blk = jnp.zeros((HG_SUB, HEAD_DIM), F32)

    def head_chunk(g, r0):
        cols = slice(g * HEAD_DIM, (g + 1) * HEAD_DIM)
        lb = lb_ref[:, cols]
        one_m_lb = 1.0 - lb
        qp = q_ref[pl.ds(r0, HG_CHUNK), cols].astype(F32)
        z = f_ref[pl.ds(r0, HG_CHUNK), cols].astype(F32)
        v = i_ref[pl.ds(r0, HG_CHUNK), cols]
        q = qp * (1.0 / (1.0 + jnp.exp(-qp))) * k_scale
        t = jnp.exp(-jnp.abs(z))
        inv = 1.0 / (1.0 + t)
        pos = z >= 0.0
        sig = jnp.where(pos, inv, t * inv)
        nsig = jnp.where(pos, t * inv, inv)
        kk = one_m_lb * nsig
        lf = jnp.log(lb + one_m_lb * sig)
        b = _cumsum_rows(lf, lcat_ref[...]) * LOG2E
        b_last = b[HG_CHUNK - 1:HG_CHUNK, :]

        st = st_ref[g]
        qd = (q * jnp.exp2(b)).astype(BF16)
        o = lax.dot_general(qd, st.astype(BF16), (((1,), (1,)), ((), ())),
                            preferred_element_type=F32)

        rows = []
        kparts = []
        r_prev = None
        for i in range(nsub):
            lo_r = i * HG_SUB
            bi = b[lo_r:lo_r + HG_SUB, :]
            qi = q[lo_r:lo_r + HG_SUB, :]
            ki = kk[lo_r:lo_r + HG_SUB, :]
            a_i = _hgrn2_diag(qi, ki, bi, lo_r, colid, rowid)
            if i > 0:
                r = b[lo_r - 1:lo_r, :]
                if kparts:
                    fac = jnp.exp2(r - r_prev)
                    kparts = [kp * fac for kp in kparts]
                b_prev = b[lo_r - HG_SUB:lo_r, :]
                kparts.append(kk[lo_r - HG_SUB:lo_r, :] * jnp.exp2(r - b_prev))
                r_prev = r
                kr = jnp.concatenate(kparts + [zero_blk] * (nsub - i), axis=0)
                qr = (qi * jnp.exp2(bi - r)).astype(BF16)
                a_i = a_i + lax.dot_general(qr, kr.astype(BF16),
                                            (((1,), (1,)), ((), ())),
                                            preferred_element_type=F32)
            rows.append(a_i)
        a = jnp.concatenate(rows, axis=0).astype(BF16)
        o = o + jnp.dot(a, v, preferred_element_type=F32)
        o_ref[pl.ds(r0, HG_CHUNK), cols] = o.astype(o_ref.dtype)

        fac = jnp.exp2(b_last - r_prev)
        b_tail = b[HG_CHUNK - HG_SUB:, :]
        kparts = [kp * fac for kp in kparts]
        kparts.append(kk[HG_CHUNK - HG_SUB:, :] * jnp.exp2(b_last - b_tail))
        kd = jnp.concatenate(kparts, axis=0).astype(BF16)
        upd = lax.dot_general(v, kd, (((0,), (0,)), ((), ())),
                              preferred_element_type=F32)
        st_ref[g] = st * jnp.exp2(b_last) + upd

    def chunk(c, carry):
        r0 = pl.multiple_of(c * HG_CHUNK, HG_CHUNK)
        for g in range(heads_per_step):
            head_chunk(g, r0)
        return carry

    lax.fori_loop(0, nchunks, chunk, 0)


def _hgrn2(proj, lb_all, layer, batch, seq, heads, lcat):
    m = proj.shape[0]
    t = _pick(seq, (512, 256, 128))
    nt = seq // t

    hps = _pick(heads, (HG_HEADS_PER_STEP, 1))
    width = hps * HEAD_DIM
    groups = heads // hps

    def col(off):
        return pl.BlockSpec((t, width), lambda b, h, s: (b * nt + s, off + h))

    kern = functools.partial(_hgrn2_kernel, nchunks=t // HG_CHUNK,
                             k_scale=float(HEAD_DIM) ** -0.5, heads_per_step=hps)
    return pl.pallas_call(
        kern,
        grid=(batch, groups, nt),
        in_specs=[col(0), col(groups), col(2 * groups),
                  pl.BlockSpec((None, 1, width), lambda b, h, s: (layer, 0, h)),
                  pl.BlockSpec(lcat.shape, lambda b, h, s: (0, 0))],
        out_specs=pl.BlockSpec((t, width), lambda b, h, s: (b * nt + s, h)),
        out_shape=jax.ShapeDtypeStruct((m, heads * HEAD_DIM), BF16),
        scratch_shapes=[pltpu.VMEM((hps, HEAD_DIM, HEAD_DIM), F32)],
        compiler_params=_params("parallel", "parallel", "arbitrary"),
        name="hgrn2",
    )(proj, proj, proj, lb_all, lcat)


def _fox_prep_kernel(ff_ref, bias_ref, fq_ref, fk_ref, lcat_ref, pq_ref, pk_ref,
                     cq_ref, ck_ref, qcat_ref, kcat_ref, carry_ref, *, heads):
    @pl.when(pl.program_id(1) == 0)
    def _():
        carry_ref[...] = jnp.zeros_like(carry_ref)

    x = ff_ref[...] + bias_ref[...]
    lf = jnp.minimum(x, 0.0) - jnp.log(1.0 + jnp.exp(-jnp.abs(x)))
    c = _cumsum_rows(lf, lcat_ref[...]) + carry_ref[...]
    carry_ref[...] = c[c.shape[0] - 1:, :]
    e = jnp.concatenate(_split3(c * LOG2E), axis=1)
    qe = jnp.dot(e, pq_ref[...], preferred_element_type=F32) + cq_ref[...]
    ke = jnp.dot(e, pk_ref[...], preferred_element_type=F32) + ck_ref[...]
    for h in range(heads):
        src = slice(h * HEAD_DIM, (h + 1) * HEAD_DIM)
        lo = 2 * h * HEAD_DIM
        qcat_ref[:, lo:lo + HEAD_DIM] = fq_ref[:, src]
        qcat_ref[:, lo + HEAD_DIM:lo + 2 * HEAD_DIM] = qe[:, src].astype(qcat_ref.dtype)
        kcat_ref[:, lo:lo + HEAD_DIM] = fk_ref[:, src]
        kcat_ref[:, lo + HEAD_DIM:lo + 2 * HEAD_DIM] = ke[:, src].astype(kcat_ref.dtype)


def _fox_expanders(heads):
    width = heads * HEAD_DIM
    pq = np.zeros((3 * LANES, width), np.float32)
    pk = np.zeros((3 * LANES, width), np.float32)
    cq = np.zeros((1, width), np.float32)
    ck = np.zeros((1, width), np.float32)
    for h in range(heads):
        for part in range(3):
            pq[part * LANES + h, h * HEAD_DIM + part] = 1.0
            pk[part * LANES + h, h * HEAD_DIM + 3 + part] = -1.0
            cq[0, h * HEAD_DIM + 3 + part] = 1.0
            ck[0, h * HEAD_DIM + part] = 1.0
    return (jnp.asarray(pq, BF16), jnp.asarray(pk, BF16),
            jnp.asarray(cq, F32), jnp.asarray(ck, F32))


def _fox_prep(ff, bias_row, proj, fq_slab, fk_slab, batch, seq, heads):
    m = ff.shape[0]
    blk = _pick(seq, (256, 128))
    nb = seq // blk
    width = heads * HEAD_DIM
    lcat = _tri_cat(blk)
    pq, pk, cq, ck = _fox_expanders(heads)
    full = lambda a: pl.BlockSpec(a.shape, lambda b, s: (0, 0))
    slab = lambda cb: pl.BlockSpec((blk, width), lambda b, s: (b * nb + s, cb))
    out_spec = pl.BlockSpec((blk, 2 * width), lambda b, s: (b * nb + s, 0))
    return pl.pallas_call(
        functools.partial(_fox_prep_kernel, heads=heads),
        grid=(batch, nb),
        in_specs=[pl.BlockSpec((blk, LANES), lambda b, s: (b * nb + s, 0)),
                  full(bias_row), slab(fq_slab), slab(fk_slab),
                  full(lcat), full(pq), full(pk), full(cq), full(ck)],
        out_specs=[out_spec, out_spec],
        out_shape=[jax.ShapeDtypeStruct((m, 2 * width), BF16)] * 2,
        scratch_shapes=[pltpu.VMEM((1, LANES), F32)],
        compiler_params=_params("parallel", "arbitrary"),
        name="fox_prep",
    )(ff, bias_row, proj, proj, lcat, pq, pk, cq, ck)


def _rope(t, cos_t, sin_t):
    half = MLA_ROPE // 2
    lane = lax.broadcasted_iota(jnp.int32, t.shape, 1)
    swapped = jnp.where(lane < half,
                        pltpu.roll(t, LANES - half, 1),
                        pltpu.roll(t, half, 1))
    return t * cos_t + swapped * sin_t


def _flash_kernel(qi_ref, kj_ref, *refs, rope_q, ratio):
    if rope_q:
        (q_ref, cos_ref, sin_ref, k_ref, v_ref, o_ref,
         qf_ref, va_ref, m_ref, acc_ref) = refs
    else:
        q_ref, k_ref, v_ref, o_ref, va_ref, m_ref, acc_ref = refs
    tq = q_ref.shape[0]
    tk = k_ref.shape[0]
    p_idx = pl.program_id(2)
    qi = qi_ref[p_idx]
    kj = kj_ref[p_idx]

    @pl.when(kj == 0)
    def _():
        if rope_q:
            qf_ref[:, :HEAD_DIM] = q_ref[:, :HEAD_DIM]
            q2 = _rope(q_ref[:, HEAD_DIM:].astype(F32), cos_ref[...], sin_ref[...])
            qf_ref[:, HEAD_DIM:] = q2.astype(qf_ref.dtype)
        va_ref[:, HEAD_DIM:] = jnp.ones((tk, HEAD_DIM), va_ref.dtype)
        m_ref[...] = jnp.full_like(m_ref, MASK_VALUE)
        acc_ref[...] = jnp.zeros_like(acc_ref)

    def step(masked):
        qsrc = qf_ref if rope_q else q_ref
        va_ref[:, :HEAD_DIM] = v_ref[...]
        rows = tq // FLASH_CHAINS

        diag = masked and ratio == 1
        span = tk // FLASH_KV_SPLIT

        stages = []
        for r in range(FLASH_CHAINS):
            kv_len = (r + 1) * rows if diag else tk
            stages += [(r, c0, min(c0 + span, kv_len)) for c0 in range(0, kv_len, span)]

        def logits(stage):
            r, c0, c1 = stage
            return lax.dot_general(qsrc[r * rows:(r + 1) * rows, :], k_ref[c0:c1, :],
                                   (((1,), (1,)), ((), ())),
                                   preferred_element_type=F32)

        s_next = logits(stages[0])
        for idx, (r, c0, c1) in enumerate(stages):
            s = s_next
            if idx + 1 < len(stages):
                s_next = logits(stages[idx + 1])
            sl = slice(r * rows, (r + 1) * rows)
            if diag and c1 > r * rows:
                d0 = r * rows - c0
                tri = (lax.broadcasted_iota(jnp.int32, (rows, rows), 1)
                       <= lax.broadcasted_iota(jnp.int32, (rows, rows), 0))
                s_diag = jnp.where(tri, s[:, d0:], MASK_VALUE)
                s = s_diag if d0 == 0 else jnp.concatenate([s[:, :d0], s_diag], axis=1)
            elif masked and not diag:
                row = lax.broadcasted_iota(jnp.int32, s.shape, 0) + (qi * tq + r * rows)
                col = lax.broadcasted_iota(jnp.int32, s.shape, 1) + (kj * tk + c0)
                s = jnp.where(col <= row, s, MASK_VALUE)
            m_prev = m_ref[sl, :]
            m_new = jnp.maximum(m_prev, jnp.max(s, axis=-1, keepdims=True))
            alpha = jnp.exp2(m_prev - m_new)
            p = jnp.exp2(s - jnp.tile(m_new, (1, s.shape[1] // LANES)))
            acc_ref[sl, :] = jnp.tile(alpha, (1, 2)) * acc_ref[sl, :] + jnp.dot(
                p.astype(va_ref.dtype), va_ref[c0:c1, :], preferred_element_type=F32)
            m_ref[sl, :] = m_new

    @pl.when(kj < qi * ratio)
    def _():
        step(False)

    @pl.when(kj >= qi * ratio)
    def _():
        step(True)

    @pl.when(kj == (qi + 1) * ratio - 1)
    def _():
        acc = acc_ref[...]
        o_ref[...] = (acc[:, :HEAD_DIM] / acc[:, HEAD_DIM:]).astype(o_ref.dtype)


def _flash(q, k, k_off, v, v_off, batch, seq, heads, rope=None):
    m = q.shape[0]
    tq = _pick(seq, (FLASH_TQ, 512, 256, 128))
    tk = _pick(tq, (FLASH_TK, 256, 128))
    nq = seq // tq
    nk = seq // tk
    ratio = tq // tk
    pairs = [(i, j) for i in range(nq) for j in range((i + 1) * ratio)]
    qi_tab = jnp.asarray([p[0] for p in pairs], jnp.int32)
    kj_tab = jnp.asarray([p[1] for p in pairs], jnp.int32)

    in_specs = [pl.BlockSpec((tq, 2 * HEAD_DIM),
                             lambda b, h, p, qi, kj: (b * nq + qi[p], h))]
    args = [q]
    scratch = []
    if rope is not None:
        tab = pl.BlockSpec((tq, LANES), lambda b, h, p, qi, kj: (b * nq + qi[p], 0))
        in_specs += [tab, tab]
        args += list(rope)
        scratch.append(pltpu.VMEM((tq, 2 * HEAD_DIM), BF16))
    in_specs += [pl.BlockSpec((tk, 2 * HEAD_DIM),
                              lambda b, h, p, qi, kj: (b * nk + kj[p], k_off + h)),
                 pl.BlockSpec((tk, HEAD_DIM),
                              lambda b, h, p, qi, kj: (b * nk + kj[p], v_off + h))]
    args += [k, v]
    scratch += [pltpu.VMEM((tk, 2 * HEAD_DIM), BF16),
                pltpu.VMEM((tq, LANES), F32),
                pltpu.VMEM((tq, 2 * HEAD_DIM), F32)]
    return pl.pallas_call(
        functools.partial(_flash_kernel, rope_q=rope is not None, ratio=ratio),
        grid_spec=pltpu.PrefetchScalarGridSpec(
            num_scalar_prefetch=2,
            grid=(batch, heads, len(pairs)),
            in_specs=in_specs,
            out_specs=pl.BlockSpec((tq, HEAD_DIM),
                                   lambda b, h, p, qi, kj: (b * nq + qi[p], h)),
            scratch_shapes=scratch),
        out_shape=jax.ShapeDtypeStruct((m, heads * HEAD_DIM), BF16),
        compiler_params=_params("parallel", "parallel", "arbitrary"),
        name="flash_rope" if rope is not None else "flash",
    )(qi_tab, kj_tab, *args)


def _mla_prep_kernel(mq_ref, mkv_ref, qw_ref, kvw_ref, cos_ref, sin_ref,
                     qn_ref, kvin_ref, *, kv_rank):
    mq = mq_ref[...].astype(F32)
    ms = jnp.mean(mq * mq, axis=-1, keepdims=True)
    qn_ref[...] = (mq * lax.rsqrt(ms + EPS) * qw_ref[...]).astype(qn_ref.dtype)
    lat = mkv_ref[:, :kv_rank].astype(F32)
    ms = jnp.mean(lat * lat, axis=-1, keepdims=True)
    kvin_ref[:, :kv_rank] = (lat * lax.rsqrt(ms + EPS) * kvw_ref[...]).astype(kvin_ref.dtype)
    kpe = mkv_ref[:, kv_rank:kv_rank + LANES].astype(F32)
    kvin_ref[:, kv_rank:] = _rope(kpe, cos_ref[...], sin_ref[...]).astype(kvin_ref.dtype)


def _mla_prep(proj, mq_blk, mkv_blk, q_rank, kv_rank, qw, kvw, cos_t, sin_t):
    m = proj.shape[0]
    tm = _pick(m, (256, 128, 64, 32, 16, 8))
    row = lambda width, blk: pl.BlockSpec((tm, width), lambda i: (i, blk))
    full = lambda a: pl.BlockSpec(a.shape, lambda i: (0, 0))
    return pl.pallas_call(
        functools.partial(_mla_prep_kernel, kv_rank=kv_rank),
        grid=(m // tm,),
        in_specs=[row(q_rank, mq_blk), row(q_rank, mkv_blk), full(qw), full(kvw),
                  row(LANES, 0), row(LANES, 0)],
        out_specs=[row(q_rank, 0), row(kv_rank + LANES, 0)],
        out_shape=[jax.ShapeDtypeStruct((m, q_rank), BF16),
                   jax.ShapeDtypeStruct((m, kv_rank + LANES), BF16)],
        compiler_params=_params("parallel"),
        name="mla_prep",
    )(proj, proj, qw, kvw, cos_t, sin_t)


def _postmix_kernel(ohg_ref, g_ref, ofox_ref, omla_ref, hgw_ref, fw_ref, mw_ref,
                    y_ref, *, hg_w, fox_w):
    def normed(o_ref, w_ref):
        o = o_ref[...].astype(F32)
        ms = jnp.mean(o * o, axis=-1, keepdims=True)
        return o * lax.rsqrt(ms + EPS) * w_ref[...]

    g = g_ref[...].astype(F32)
    y_hg = normed(ohg_ref, hgw_ref) * (g * (1.0 / (1.0 + jnp.exp(-g))))
    y_ref[:, :hg_w] = y_hg.astype(y_ref.dtype)
    y_ref[:, hg_w:hg_w + fox_w] = normed(ofox_ref, fw_ref).astype(y_ref.dtype)
    y_ref[:, hg_w + fox_w:] = normed(omla_ref, mw_ref).astype(y_ref.dtype)


def _postmix(o_hg, proj, g_blk, o_fox, o_mla, hgw, fw, mw):
    m, hg_w = o_hg.shape
    fox_w = o_fox.shape[1]
    mla_w = o_mla.shape[1]
    tm = _pick(m, (256, 128, 64, 32, 16, 8))
    row = lambda width, blk: pl.BlockSpec((tm, width), lambda i: (i, blk))
    full = lambda a: pl.BlockSpec(a.shape, lambda i: (0, 0))
    return pl.pallas_call(
        functools.partial(_postmix_kernel, hg_w=hg_w, fox_w=fox_w),
        grid=(m // tm,),
        in_specs=[row(hg_w, 0), row(hg_w, g_blk), row(fox_w, 0), row(mla_w, 0),
                  full(hgw), full(fw), full(mw)],
        out_specs=row(hg_w + fox_w + mla_w, 0),
        out_shape=jax.ShapeDtypeStruct((m, hg_w + fox_w + mla_w), BF16),
        compiler_params=_params("parallel"),
        name="postmix",
    )(o_hg, proj, o_fox, o_mla, hgw, fw, mw)


def kernel(x, positions, mix_norm, w_in, hg_lower_bounds, hg_out_norm, fox_f_bias,
           fox_out_norm, mla_q_a_norm, mla_w_q_b, mla_kv_a_norm, mla_w_kv_b,
           mla_out_norm, w_o, ffn_norm, w_gate, w_up, w_down, final_norm):
    batch, seq, d_model = x.shape
    depth = w_in.shape[0]
    m = batch * seq
    hg_w = hg_lower_bounds.shape[1]
    hg_heads = hg_w // HEAD_DIM
    fox_heads = fox_f_bias.shape[1]
    fox_w = fox_heads * HEAD_DIM
    q_rank = mla_q_a_norm.shape[1]
    kv_rank = mla_kv_a_norm.shape[1]
    mla_heads = mla_w_q_b.shape[2] // (HEAD_DIM + MLA_ROPE)
    mla_w = mla_heads * HEAD_DIM
    assert kv_rank + LANES <= q_rank and fox_heads <= LANES
    assert hg_w == q_rank and (6 * hg_w) % fox_w == 0

    off_fox = 4 * hg_w
    off_ff = off_fox + 3 * fox_w
    off_mq = off_ff + fox_heads
    off_mkv = off_mq + q_rank
    fox_scale = float(HEAD_DIM) ** -0.5 * LOG2E
    mla_scale = float(HEAD_DIM + MLA_ROPE) ** -0.5 * LOG2E
    pad_kv = q_rank - (kv_rank + MLA_ROPE)
    col_scale = np.ones((w_in.shape[2],), np.float32)
    col_scale[off_fox:off_fox + fox_w] = fox_scale
    w_in_t = (jnp.transpose(w_in, (0, 2, 1))
              * jnp.asarray(col_scale)[None, :, None]).astype(BF16)
    w_proj = jnp.concatenate([
        w_in_t[:, :off_fox],
        w_in_t[:, off_mq:off_mkv],
        w_in_t[:, off_mkv:],
        jnp.zeros((depth, pad_kv, d_model), BF16),
        w_in_t[:, off_fox:off_ff],
    ], axis=1)
    w_ff = jnp.pad(w_in_t[:, off_ff:off_mq],
                   ((0, 0), (0, LANES - fox_heads), (0, 0)))
    blk_mq = 4
    blk_mkv = 5
    slab_fq = (6 * hg_w) // fox_w
    blk_fv = (6 * hg_w + 2 * fox_w) // HEAD_DIM

    wq = mla_w_q_b.reshape(depth, q_rank, mla_heads, HEAD_DIM + MLA_ROPE) * mla_scale
    w_qb = jnp.pad(wq, ((0, 0),) * 3 + ((0, HEAD_DIM - MLA_ROPE),)).reshape(
        depth, q_rank, 2 * mla_w).astype(BF16)
    wkv = mla_w_kv_b.reshape(depth, kv_rank, mla_heads, 2 * HEAD_DIM)
    k_cols = jnp.pad(wkv[..., :HEAD_DIM], ((0, 0),) * 3 + ((0, HEAD_DIM),)).reshape(
        depth, kv_rank, 2 * mla_w)
    v_cols = wkv[..., HEAD_DIM:].reshape(depth, kv_rank, mla_w)
    route = np.zeros((LANES, mla_heads, 2 * HEAD_DIM), np.float32)
    route[:, :, HEAD_DIM:] = np.eye(LANES, dtype=np.float32)[:, None, :]
    route = np.concatenate([route.reshape(LANES, 2 * mla_w),
                            np.zeros((LANES, mla_w), np.float32)], axis=1)
    w_kvb = jnp.concatenate([
        jnp.concatenate([k_cols, v_cols], axis=2),
        jnp.broadcast_to(jnp.asarray(route), (depth, LANES, 3 * mla_w)),
    ], axis=1).astype(BF16)
    w_o_b = w_o.astype(BF16)

    lb_soft = jax.nn.softmax(hg_lower_bounds.astype(F32), axis=0)
    lb_all = (jnp.cumsum(lb_soft, axis=0) - lb_soft[0]).reshape(depth, 1, hg_w)
    fox_bias = jnp.pad(fox_f_bias.astype(F32),
                       ((0, 0), (0, LANES - fox_heads))).reshape(depth, 1, LANES)

    inv_freq = ROPE_BASE ** (-jnp.arange(0, MLA_ROPE, 2, dtype=F32) / MLA_ROPE)
    angles = positions.astype(F32).reshape(m, 1) * inv_freq
    cos, sin = jnp.cos(angles), jnp.sin(angles)
    zpad = jnp.zeros((m, LANES - MLA_ROPE), F32)
    cos_t = jnp.concatenate([cos, cos, zpad], axis=1)
    sin_t = jnp.concatenate([-sin, sin, zpad], axis=1)

    hg_lcat = _tri_cat(HG_CHUNK)
    xr = x.reshape(m, d_model).astype(F32)
    mix_gain = mix_norm.astype(F32).reshape(depth, 1, d_model)
    ffn_gain = ffn_norm.astype(F32).reshape(depth, 1, d_model)
    xb, rstd = _xprep(xr, mix_gain[0])

    for l in range(depth):
        proj = _matmul_norm(xb, rstd, w_proj, l, BF16, 2048, 512, name="in_proj")
        ff = _matmul_norm(xb, rstd, w_ff, l, F32, 1024, LANES, name="ff_proj")

        o_hg = _hgrn2(proj, lb_all, l, batch, seq, hg_heads, hg_lcat)

        qcat, kcat = _fox_prep(ff, fox_bias[l], proj, slab_fq, slab_fq + 1,
                               batch, seq, fox_heads)
        o_fox = _flash(qcat, kcat, 0, proj, blk_fv, batch, seq, fox_heads)

        qn, kvin = _mla_prep(proj, blk_mq, blk_mkv, q_rank, kv_rank,
                             mla_q_a_norm[l].reshape(1, q_rank).astype(F32),
                             mla_kv_a_norm[l].reshape(1, kv_rank).astype(F32),
                             cos_t, sin_t)
        q_mla = _matmul(qn, w_qb, l, BF16, 2048, mla_w, name="mla_q_b")
        kv_mla = _matmul(kvin, w_kvb, l, BF16, 2048, mla_w, name="mla_kv_b")
        o_mla = _flash(q_mla, kv_mla, 0, kv_mla, 2 * mla_heads, batch, seq,
                       mla_heads, rope=(cos_t, sin_t))

        y = _postmix(o_hg, proj, 3, o_fox, o_mla,
                     hg_out_norm[l].reshape(1, hg_w).astype(F32),
                     fox_out_norm[l].reshape(1, fox_w).astype(F32),
                     mla_out_norm[l].reshape(1, mla_w).astype(F32))
        xr, xb, rstd = _matmul_res(y, w_o_b, l, xr, ffn_gain[l], 1024, 512,
                                   name="out_proj")

        act, w_down_l = _swiglu_up(xb, rstd, w_gate, w_up, w_down, l, 1024, 256)
        xr, xb, rstd = _matmul_res(act, w_down_l, 0, xr, mix_gain[(l + 1) % depth],
                                   512, 512, name="down_proj")

    out = _rmsnorm(xr, final_norm, x.dtype)
    return out.reshape(batch, seq, d_model)
```

```python
import functools
import math

import numpy as np
import jax
import jax.numpy as jnp
from jax import lax
from jax.experimental import pallas as pl
from jax.experimental.pallas import tpu as pltpu

F32 = jnp.float32
BF16 = jnp.bfloat16

LANES = 128
HEAD_DIM = 128
MLA_ROPE = 64
ROPE_BASE = 10000.0
EPS = 1e-6
MASK_VALUE = -1e30
LOG2E = math.log2(math.e)
HG_CHUNK = 128
HG_SUB = 16
HG_HEADS_PER_STEP = 4
FLASH_TQ = 2048
FLASH_TK = 2048
MM_CHUNKS = 2
FLASH_CHAINS = 4
FLASH_KV_SPLIT = 1
VMEM_LIMIT_BYTES = 52 * 1024 * 1024


def _params(*semantics):
    return pltpu.CompilerParams(dimension_semantics=semantics,
                                vmem_limit_bytes=VMEM_LIMIT_BYTES)


def _pick(n, candidates):
    for c in candidates:
        if n % c == 0:
            return c
    return n


def _rmsnorm_kernel(x_ref, w_ref, o_ref):
    x = x_ref[...].astype(F32)
    ms = jnp.mean(x * x, axis=-1, keepdims=True)
    o_ref[...] = (x * lax.rsqrt(ms + EPS) * w_ref[...]).astype(o_ref.dtype)


def _rmsnorm(x, w, out_dtype):
    m, d = x.shape
    tm = _pick(m, (256, 128, 64, 32, 16, 8))
    return pl.pallas_call(
        _rmsnorm_kernel,
        grid=(m // tm,),
        in_specs=[pl.BlockSpec((tm, d), lambda i: (i, 0)),
                  pl.BlockSpec((1, d), lambda i: (0, 0))],
        out_specs=pl.BlockSpec((tm, d), lambda i: (i, 0)),
        out_shape=jax.ShapeDtypeStruct((m, d), out_dtype),
        compiler_params=_params("parallel"),
        name="rmsnorm",
    )(x, w.reshape(1, d).astype(F32))


def _rstd_tile(rstd_ref, width):
    return jnp.tile(rstd_ref[...], (1, width // LANES))


def _dot(a, w, trans_w):
    if trans_w:
        return lax.dot_general(a, w, (((1,), (1,)), ((), ())),
                               preferred_element_type=F32)
    return jnp.dot(a, w, preferred_element_type=F32)


def _mm_kernel(a_ref, w_ref, o_ref):
    o_ref[...] = _dot(a_ref[...], w_ref[...], False).astype(o_ref.dtype)


def _mm_norm_kernel(a_ref, rstd_ref, w_ref, o_ref, *, trans_w):
    acc = _dot(a_ref[...], w_ref[...], trans_w)
    o_ref[...] = (acc * _rstd_tile(rstd_ref, acc.shape[1])).astype(o_ref.dtype)


def _mm_swiglu_kernel(a_ref, rstd_ref, wg_ref, wu_ref, wd_ref, o_ref, wdb_ref):
    wdb_ref[...] = wd_ref[...].astype(wdb_ref.dtype)
    rows = a_ref.shape[0] // MM_CHUNKS
    wg = wg_ref[...].astype(a_ref.dtype)
    wu = wu_ref[...].astype(a_ref.dtype)

    def dots(r):
        a = a_ref[r * rows:(r + 1) * rows, :]
        return (jnp.dot(a, wg, preferred_element_type=F32),
                jnp.dot(a, wu, preferred_element_type=F32))

    nxt = dots(0)
    for r in range(MM_CHUNKS):
        g, u = nxt
        if r + 1 < MM_CHUNKS:
            nxt = dots(r + 1)
        sl = slice(r * rows, (r + 1) * rows)
        rstd = jnp.tile(rstd_ref[sl, :], (1, o_ref.shape[1] // LANES))
        g = g * rstd
        u = u * rstd
        o_ref[sl, :] = (g * (1.0 / (1.0 + jnp.exp(-g))) * u).astype(o_ref.dtype)


def _mm_res_kernel(a_ref, w_ref, r_ref, gain_ref, o_ref, ob_ref, rstd_ref, *, width):
    x = r_ref[...] + jnp.dot(a_ref[...], w_ref[...], preferred_element_type=F32)
    o_ref[...] = x
    ob_ref[...] = (x * gain_ref[...]).astype(ob_ref.dtype)
    part = jnp.broadcast_to(jnp.sum(x * x, axis=-1, keepdims=True), rstd_ref.shape)
    j = pl.program_id(1)

    @pl.when(j == 0)
    def _():
        rstd_ref[...] = part

    @pl.when(j > 0)
    def _():
        rstd_ref[...] += part

    @pl.when(j == pl.num_programs(1) - 1)
    def _():
        rstd_ref[...] = lax.rsqrt(rstd_ref[...] * (1.0 / width) + EPS)


def _xprep_kernel(x_ref, gain_ref, xb_ref, rstd_ref):
    x = x_ref[...]
    xb_ref[...] = (x * gain_ref[...]).astype(xb_ref.dtype)
    ms = jnp.mean(x * x, axis=-1, keepdims=True)
    rstd_ref[...] = jnp.broadcast_to(lax.rsqrt(ms + EPS), rstd_ref.shape)


def _xprep(x, gain):
    m, d = x.shape
    tm = _pick(m, (256, 128, 64, 32, 16, 8))
    return pl.pallas_call(
        _xprep_kernel,
        grid=(m // tm,),
        in_specs=[pl.BlockSpec((tm, d), lambda i: (i, 0)),
                  pl.BlockSpec((1, d), lambda i: (0, 0))],
        out_specs=[pl.BlockSpec((tm, d), lambda i: (i, 0)),
                   pl.BlockSpec((tm, LANES), lambda i: (i, 0))],
        out_shape=[jax.ShapeDtypeStruct((m, d), BF16),
                   jax.ShapeDtypeStruct((m, LANES), F32)],
        compiler_params=_params("parallel"),
        name="xprep",
    )(x, gain)


def _row_tile(m, tm):
    return _pick(m, (tm, 512, 256, 128, 64, 32, 16, 8))


def _matmul(a, w, layer, out_dtype, tm, tn, name):
    m, k = a.shape
    n = w.shape[-1]
    tm = _row_tile(m, tm)
    tn = _pick(n, (tn, 512, 256, 128))
    return pl.pallas_call(
        _mm_kernel,
        grid=(m // tm, n // tn),
        in_specs=[pl.BlockSpec((tm, k), lambda i, j: (i, 0)),
                  pl.BlockSpec((None, k, tn), lambda i, j: (layer, 0, j))],
        out_specs=pl.BlockSpec((tm, tn), lambda i, j: (i, j)),
        out_shape=jax.ShapeDtypeStruct((m, n), out_dtype),
        compiler_params=_params("parallel", "arbitrary"),
        name=name,
    )(a, w)


def _mm_norm_cast_kernel(a_ref, rstd_ref, w_ref, src_ref, o_ref, dst_ref):
    dst_ref[...] = src_ref[...].astype(dst_ref.dtype)
    _mm_norm_kernel(a_ref, rstd_ref, w_ref, o_ref, trans_w=True)


def _matmul_norm(a, rstd, w_t, layer, out_dtype, tm, tn, name, n_cols=None, cast_src=None):
    m, k = a.shape
    n = n_cols or w_t.shape[1]
    tm = _row_tile(m, tm)
    tn = _pick(n, (tn, 512, 256, 128))
    ni, nj = m // tm, n // tn
    in_specs = [pl.BlockSpec((tm, k), lambda i, j: (i, 0)),
                pl.BlockSpec((tm, LANES), lambda i, j: (i, 0)),
                pl.BlockSpec((None, tn, k), lambda i, j: (layer, j, 0))]
    out_specs = pl.BlockSpec((tm, tn), lambda i, j: (i, j))
    out_shape = jax.ShapeDtypeStruct((m, n), out_dtype)
    if cast_src is None:
        kern, args = functools.partial(_mm_norm_kernel, trans_w=True), (a, rstd, w_t)
    else:
        rows, cols = cast_src.shape[1:]
        slab = rows // (ni * nj)
        assert slab * ni * nj == rows and slab % 16 == 0
        in_specs.append(pl.BlockSpec((None, slab, cols), lambda i, j: (layer, i * nj + j, 0)))
        out_specs = [out_specs,
                     pl.BlockSpec((None, slab, cols), lambda i, j: (0, i * nj + j, 0))]
        out_shape = [out_shape, jax.ShapeDtypeStruct((1, rows, cols), BF16)]
        kern, args = _mm_norm_cast_kernel, (a, rstd, w_t, cast_src)
    return pl.pallas_call(
        kern,
        grid=(ni, nj),
        in_specs=in_specs,
        out_specs=out_specs,
        out_shape=out_shape,
        compiler_params=_params("parallel", "arbitrary"),
        name=name,
    )(*args)


def _matmul_res(a, w, layer, residual, gain, tm, tn, name):
    m, k = a.shape
    n = w.shape[-1]
    tm = _row_tile(m, tm)
    tn = _pick(n, (tn, 512, 256, 128))
    tile = pl.BlockSpec((tm, tn), lambda i, j: (i, j))
    return pl.pallas_call(
        functools.partial(_mm_res_kernel, width=n),
        grid=(m // tm, n // tn),
        in_specs=[pl.BlockSpec((tm, k), lambda i, j: (i, 0)),
                  pl.BlockSpec((None, k, tn), lambda i, j: (layer, 0, j)),
                  tile,
                  pl.BlockSpec((1, tn), lambda i, j: (0, j))],
        out_specs=[tile, tile, pl.BlockSpec((tm, LANES), lambda i, j: (i, 0))],
        out_shape=[jax.ShapeDtypeStruct((m, n), F32),
                   jax.ShapeDtypeStruct((m, n), BF16),
                   jax.ShapeDtypeStruct((m, LANES), F32)],
        compiler_params=_params("parallel", "arbitrary"),
        name=name,
    )(a, w, residual, gain)


def _swiglu_up(a, rstd, wg, wu, wd, layer, tm, tn):
    m, k = a.shape
    n = wg.shape[-1]
    tm = _row_tile(m, tm)
    tn = _pick(n, (tn, 256, 128))
    ni, nj = m // tm, n // tn
    slab = wd.shape[1] // (ni * nj)
    assert slab * ni * nj == wd.shape[1] and slab % 16 == 0
    w_spec = pl.BlockSpec((None, k, tn), lambda i, j: (layer, 0, j))
    return pl.pallas_call(
        _mm_swiglu_kernel,
        grid=(ni, nj),
        in_specs=[pl.BlockSpec((tm, k), lambda i, j: (i, 0)),
                  pl.BlockSpec((tm, LANES), lambda i, j: (i, 0)), w_spec, w_spec,
                  pl.BlockSpec((None, slab, wd.shape[2]),
                               lambda i, j: (layer, i * nj + j, 0))],
        out_specs=[pl.BlockSpec((tm, tn), lambda i, j: (i, j)),
                   pl.BlockSpec((None, slab, wd.shape[2]), lambda i, j: (0, i * nj + j, 0))],
        out_shape=[jax.ShapeDtypeStruct((m, n), BF16),
                   jax.ShapeDtypeStruct((1,) + wd.shape[1:], BF16)],
        compiler_params=_params("parallel", "arbitrary"),
        name="swiglu_up",
    )(a, rstd, wg, wu, wd)


def _split3(x):
    hi = x.astype(BF16)
    r1 = x - hi.astype(F32)
    mid = r1.astype(BF16)
    lo = (r1 - mid.astype(F32)).astype(BF16)
    return hi, mid, lo


def _tri_cat(n):
    tri = np.tril(np.ones((n, n), np.float32))
    return jnp.asarray(np.concatenate([tri, tri, tri], axis=1), dtype=BF16)


def _cumsum_rows(x, lcat):
    hi, mid, lo = _split3(x)
    stacked = jnp.concatenate([hi, mid, lo], axis=0)
    return jnp.dot(lcat, stacked, preferred_element_type=F32)


def _hgrn2_diag(qi, ki, bi, base, colid, rowid):
    half = HG_SUB // 2
    halves = []
    for t0 in (0, half):
        bt = bi[t0:t0 + half, :]
        qt = qi[t0:t0 + half, :]
        cid = colid[:half, :]
        a = jnp.zeros(cid.shape, F32)
        for s in range(HG_SUB if t0 else half):
            d = jnp.exp2(bt - bi[s:s + 1, :])
            col = jnp.sum(d * (qt * ki[s:s + 1, :]), axis=-1, keepdims=True)
            a = jnp.where(cid == base + s, col, a)
        halves.append(a)
    a = jnp.concatenate(halves, axis=0)
    return jnp.where(colid <= rowid + base, a, 0.0)


def _hgrn2_kernel(q_ref, f_ref, i_ref, lb_ref, lcat_ref, o_ref, st_ref, *,
                  nchunks, k_scale, heads_per_step):
    @pl.when(pl.program_id(2) == 0)
    def _():
        st_ref[...] = jnp.zeros_like(st_ref)

    nsub = HG_CHUNK // HG_SUB
    colid = lax.broadcasted_iota(jnp.int32, (HG_SUB, HG_CHUNK), 1)
    rowid = lax.broadcasted_iota(jnp.int32, (HG_SUB, HG_CHUNK), 0)
    zero_blk = jnp.zeros((HG_SUB, HEAD_DIM), F32)

    def head_chunk(g, r0):
        cols = slice(g * HEAD_DIM, (g + 1) * HEAD_DIM)
        lb = lb_ref[:, cols]
        one_m_lb = 1.0 - lb
        qp = q_ref[pl.ds(r0, HG_CHUNK), cols].astype(F32)
        z = f_ref[pl.ds(r0, HG_CHUNK), cols].astype(F32)
        v = i_ref[pl.ds(r0, HG_CHUNK), cols]
        q = qp * (1.0 / (1.0 + jnp.exp(-qp))) * k_scale
        t = jnp.exp(-jnp.abs(z))
        inv = 1.0 / (1.0 + t)
        pos = z >= 0.0
        sig = jnp.where(pos, inv, t * inv)
        nsig = jnp.where(pos, t * inv, inv)
        kk = one_m_lb * nsig
        lf = jnp.log(lb + one_m_lb * sig)
        b = _cumsum_rows(lf, lcat_ref[...]) * LOG2E
        b_last = b[HG_CHUNK - 1:HG_CHUNK, :]

        st = st_ref[g]
        qd = (q * jnp.exp2(b)).astype(BF16)
        o = lax.dot_general(qd, st.astype(BF16), (((1,), (1,)), ((), ())),
                            preferred_element_type=F32)

        rows = []
        kparts = []
        r_prev = None
        for i in range(nsub):
            lo_r = i * HG_SUB
            bi = b[lo_r:lo_r + HG_SUB, :]
            qi = q[lo_r:lo_r + HG_SUB, :]
            ki = kk[lo_r:lo_r + HG_SUB, :]
            a_i = _hgrn2_diag(qi, ki, bi, lo_r, colid, rowid)
            if i > 0:
                r = b[lo_r - 1:lo_r, :]
                if kparts:
                    fac = jnp.exp2(r - r_prev)
                    kparts = [kp * fac for kp in kparts]
                b_prev = b[lo_r - HG_SUB:lo_r, :]
                kparts.append(kk[lo_r - HG_SUB:lo_r, :] * jnp.exp2(r - b_prev))
                r_prev = r
                kr = jnp.concatenate(kparts + [zero_blk] * (nsub - i), axis=0)
                qr = (qi * jnp.exp2(bi - r)).astype(BF16)
                a_i = a_i + lax.dot_general(qr, kr.astype(BF16),
                                            (((1,), (1,)), ((), ())),
                                            preferred_element_type=F32)
            rows.append(a_i)
        a = jnp.concatenate(rows, axis=0).astype(BF16)
        o = o + jnp.dot(a, v, preferred_element_type=F32)
        o_ref[pl.ds(r0, HG_CHUNK), cols] = o.astype(o_ref.dtype)

        fac = jnp.exp2(b_last - r_prev)
        b_tail = b[HG_CHUNK - HG_SUB:, :]
        kparts = [kp * fac for kp in kparts]
        kparts.append(kk[HG_CHUNK - HG_SUB:, :] * jnp.exp2(b_last - b_tail))
        kd = jnp.concatenate(kparts, axis=0).astype(BF16)
        upd = lax.dot_general(v, kd, (((0,), (0,)), ((), ())),
                              preferred_element_type=F32)
        st_ref[g] = st * jnp.exp2(b_last) + upd

    def chunk(c, carry):
        r0 = pl.multiple_of(c * HG_CHUNK, HG_CHUNK)
        for g in range(heads_per_step):
            head_chunk(g, r0)
        return carry

    lax.fori_loop(0, nchunks, chunk, 0)


def _hgrn2(proj, lb_all, layer, batch, seq, heads, lcat):
    m = proj.shape[0]
    t = _pick(seq, (512, 256, 128))
    nt = seq // t

    hps = _pick(heads, (HG_HEADS_PER_STEP, 1))
    width = hps * HEAD_DIM
    groups = heads // hps

    def col(off):
        return pl.BlockSpec((t, width), lambda b, h, s: (b * nt + s, off + h))

    kern = functools.partial(_hgrn2_kernel, nchunks=t // HG_CHUNK,
                             k_scale=float(HEAD_DIM) ** -0.5, heads_per_step=hps)
    return pl.pallas_call(
        kern,
        grid=(batch, groups, nt),
        in_specs=[col(0), col(groups), col(2 * groups),
                  pl.BlockSpec((None, 1, width), lambda b, h, s: (layer, 0, h)),
                  pl.BlockSpec(lcat.shape, lambda b, h, s: (0, 0))],
        out_specs=pl.BlockSpec((t, width), lambda b, h, s: (b * nt + s, h)),
        out_shape=jax.ShapeDtypeStruct((m, heads * HEAD_DIM), BF16),
        scratch_shapes=[pltpu.VMEM((hps, HEAD_DIM, HEAD_DIM), F32)],
        compiler_params=_params("parallel", "parallel", "arbitrary"),
        name="hgrn2",
    )(proj, proj, proj, lb_all, lcat)


def _fox_prep_kernel(ff_ref, bias_ref, lcat_ref, pq_ref, pk_ref, cq_ref, ck_ref,
                     *refs, heads, piece):
    n_pieces = 2 * heads * HEAD_DIM // piece
    qk_refs = refs[:n_pieces]
    qcat_ref, kcat_ref, carry_ref = refs[n_pieces:]

    def head_cols(col):
        return qk_refs[col // piece][:, col % piece:col % piece + HEAD_DIM]

    @pl.when(pl.program_id(1) == 0)
    def _():
        carry_ref[...] = jnp.zeros_like(carry_ref)

    x = ff_ref[...] + bias_ref[...]
    lf = jnp.minimum(x, 0.0) - jnp.log(1.0 + jnp.exp(-jnp.abs(x)))
    c = _cumsum_rows(lf, lcat_ref[...]) + carry_ref[...]
    carry_ref[...] = c[c.shape[0] - 1:, :]
    e = jnp.concatenate(_split3(c * LOG2E), axis=1)
    qe = jnp.dot(e, pq_ref[...], preferred_element_type=F32) + cq_ref[...]
    ke = jnp.dot(e, pk_ref[...], preferred_element_type=F32) + ck_ref[...]
    for h in range(heads):
        src = slice(h * HEAD_DIM, (h + 1) * HEAD_DIM)
        lo = 2 * h * HEAD_DIM
        qcat_ref[:, lo:lo + HEAD_DIM] = head_cols(h * HEAD_DIM)
        qcat_ref[:, lo + HEAD_DIM:lo + 2 * HEAD_DIM] = qe[:, src].astype(qcat_ref.dtype)
        kcat_ref[:, lo:lo + HEAD_DIM] = head_cols((heads + h) * HEAD_DIM)
        kcat_ref[:, lo + HEAD_DIM:lo + 2 * HEAD_DIM] = ke[:, src].astype(kcat_ref.dtype)


def _fox_expanders(heads):
    width = heads * HEAD_DIM
    pq = np.zeros((3 * LANES, width), np.float32)
    pk = np.zeros((3 * LANES, width), np.float32)
    cq = np.zeros((1, width), np.float32)
    ck = np.zeros((1, width), np.float32)
    for h in range(heads):
        for part in range(3):
            pq[part * LANES + h, h * HEAD_DIM + part] = 1.0
            pk[part * LANES + h, h * HEAD_DIM + 3 + part] = -1.0
            cq[0, h * HEAD_DIM + 3 + part] = 1.0
            ck[0, h * HEAD_DIM + part] = 1.0
    return (jnp.asarray(pq, BF16), jnp.asarray(pk, BF16),
            jnp.asarray(cq, F32), jnp.asarray(ck, F32))


def _fox_prep(ff, bias_row, proj, off_q, batch, seq, heads):
    m = ff.shape[0]
    blk = _pick(seq, (256, 128))
    nb = seq // blk
    width = heads * HEAD_DIM
    piece = math.gcd(off_q, width)
    assert piece % HEAD_DIM == 0
    n_pieces = 2 * width // piece
    lcat = _tri_cat(blk)
    pq, pk, cq, ck = _fox_expanders(heads)
    full = lambda a: pl.BlockSpec(a.shape, lambda b, s: (0, 0))

    def piece_spec(p):
        return pl.BlockSpec((blk, piece), lambda b, s: (b * nb + s, off_q // piece + p))

    out_spec = pl.BlockSpec((blk, 2 * width), lambda b, s: (b * nb + s, 0))
    return pl.pallas_call(
        functools.partial(_fox_prep_kernel, heads=heads, piece=piece),
        grid=(batch, nb),
        in_specs=[pl.BlockSpec((blk, LANES), lambda b, s: (b * nb + s, 0)),
                  full(bias_row), full(lcat), full(pq), full(pk), full(cq), full(ck)]
                 + [piece_spec(p) for p in range(n_pieces)],
        out_specs=[out_spec, out_spec],
        out_shape=[jax.ShapeDtypeStruct((m, 2 * width), BF16)] * 2,
        scratch_shapes=[pltpu.VMEM((1, LANES), F32)],
        compiler_params=_params("parallel", "arbitrary"),
        name="fox_prep",
    )(ff, bias_row, lcat, pq, pk, cq, ck, *([proj] * n_pieces))


def _rope(t, cos_t, sin_t):
    half = MLA_ROPE // 2
    lane = lax.broadcasted_iota(jnp.int32, t.shape, 1)
    swapped = jnp.where(lane < half,
                        pltpu.roll(t, LANES - half, 1),
                        pltpu.roll(t, half, 1))
    return t * cos_t + swapped * sin_t


def _flash_kernel(qi_ref, kj_ref, *refs, rope_q, ratio):
    if rope_q:
        (q_ref, cos_ref, sin_ref, k_ref, v_ref, o_ref,
         qf_ref, va_ref, m_ref, acc_ref) = refs
    else:
        q_ref, k_ref, v_ref, o_ref, va_ref, m_ref, acc_ref = refs
    tq = q_ref.shape[0]
    tk = k_ref.shape[0]
    p_idx = pl.program_id(2)
    qi = qi_ref[p_idx]
    kj = kj_ref[p_idx]

    @pl.when(kj == 0)
    def _():
        if rope_q:
            qf_ref[:, :HEAD_DIM] = q_ref[:, :HEAD_DIM]
            q2 = _rope(q_ref[:, HEAD_DIM:].astype(F32), cos_ref[...], sin_ref[...])
            qf_ref[:, HEAD_DIM:] = q2.astype(qf_ref.dtype)
        va_ref[:, HEAD_DIM:] = jnp.ones((tk, HEAD_DIM), va_ref.dtype)
        m_ref[...] = jnp.full_like(m_ref, MASK_VALUE)
        acc_ref[...] = jnp.zeros_like(acc_ref)

    def step(masked):
        qsrc = qf_ref if rope_q else q_ref
        va_ref[:, :HEAD_DIM] = v_ref[...]
        rows = tq // FLASH_CHAINS

        diag = masked and ratio == 1
        span = tk // FLASH_KV_SPLIT

        stages = []
        for r in range(FLASH_CHAINS):
            kv_len = (r + 1) * rows if diag else tk
            stages += [(r, c0, min(c0 + span, kv_len)) for c0 in range(0, kv_len, span)]

        def logits(stage):
            r, c0, c1 = stage
            return lax.dot_general(qsrc[r * rows:(r + 1) * rows, :], k_ref[c0:c1, :],
                                   (((1,), (1,)), ((), ())),
                                   preferred_element_type=F32)

        s_next = logits(stages[0])
        for idx, (r, c0, c1) in enumerate(stages):
            s = s_next
            if idx + 1 < len(stages):
                s_next = logits(stages[idx + 1])
            sl = slice(r * rows, (r + 1) * rows)
            if diag and c1 > r * rows:
                d0 = r * rows - c0
                tri = (lax.broadcasted_iota(jnp.int32, (rows, rows), 1)
                       <= lax.broadcasted_iota(jnp.int32, (rows, rows), 0))
                s_diag = jnp.where(tri, s[:, d0:], MASK_VALUE)
                s = s_diag if d0 == 0 else jnp.concatenate([s[:, :d0], s_diag], axis=1)
            elif masked and not diag:
                row = lax.broadcasted_iota(jnp.int32, s.shape, 0) + (qi * tq + r * rows)
                col = lax.broadcasted_iota(jnp.int32, s.shape, 1) + (kj * tk + c0)
                s = jnp.where(col <= row, s, MASK_VALUE)
            m_prev = m_ref[sl, :]
            m_new = jnp.maximum(m_prev, jnp.max(s, axis=-1, keepdims=True))
            alpha = jnp.exp2(m_prev - m_new)
            p = jnp.exp2(s - jnp.tile(m_new, (1, s.shape[1] // LANES)))
            acc_ref[sl, :] = jnp.tile(alpha, (1, 2)) * acc_ref[sl, :] + jnp.dot(
                p.astype(va_ref.dtype), va_ref[c0:c1, :], preferred_element_type=F32)
            m_ref[sl, :] = m_new

    @pl.when(kj < qi * ratio)
    def _():
        step(False)

    @pl.when(kj >= qi * ratio)
    def _():
        step(True)

    @pl.when(kj == (qi + 1) * ratio - 1)
    def _():
        acc = acc_ref[...]
        o_ref[...] = (acc[:, :HEAD_DIM] / acc[:, HEAD_DIM:]).astype(o_ref.dtype)


def _flash(q, k, k_off, v, v_off, batch, seq, heads, rope=None):
    m = q.shape[0]
    tq = _pick(seq, (FLASH_TQ, 512, 256, 128))
    tk = _pick(tq, (FLASH_TK, 256, 128))
    nq = seq // tq
    nk = seq // tk
    ratio = tq // tk
    pairs = [(i, j) for i in range(nq) for j in range((i + 1) * ratio)]
    qi_tab = jnp.asarray([p[0] for p in pairs], jnp.int32)
    kj_tab = jnp.asarray([p[1] for p in pairs], jnp.int32)

    in_specs = [pl.BlockSpec((tq, 2 * HEAD_DIM),
                             lambda b, h, p, qi, kj: (b * nq + qi[p], h))]
    args = [q]
    scratch = []
    if rope is not None:
        tab = pl.BlockSpec((tq, LANES), lambda b, h, p, qi, kj: (b * nq + qi[p], 0))
        in_specs += [tab, tab]
        args += list(rope)
        scratch.append(pltpu.VMEM((tq, 2 * HEAD_DIM), BF16))
    in_specs += [pl.BlockSpec((tk, 2 * HEAD_DIM),
                              lambda b, h, p, qi, kj: (b * nk + kj[p], k_off + h)),
                 pl.BlockSpec((tk, HEAD_DIM),
                              lambda b, h, p, qi, kj: (b * nk + kj[p], v_off + h))]
    args += [k, v]
    scratch += [pltpu.VMEM((tk, 2 * HEAD_DIM), BF16),
                pltpu.VMEM((tq, LANES), F32),
                pltpu.VMEM((tq, 2 * HEAD_DIM), F32)]
    return pl.pallas_call(
        functools.partial(_flash_kernel, rope_q=rope is not None, ratio=ratio),
        grid_spec=pltpu.PrefetchScalarGridSpec(
            num_scalar_prefetch=2,
            grid=(batch, heads, len(pairs)),
            in_specs=in_specs,
            out_specs=pl.BlockSpec((tq, HEAD_DIM),
                                   lambda b, h, p, qi, kj: (b * nq + qi[p], h)),
            scratch_shapes=scratch),
        out_shape=jax.ShapeDtypeStruct((m, heads * HEAD_DIM), BF16),
        compiler_params=_params("parallel", "parallel", "arbitrary"),
        name="flash_rope" if rope is not None else "flash",
    )(qi_tab, kj_tab, *args)


def _mla_prep_kernel(mq_ref, mkv_ref, qw_ref, kvw_ref, cos_ref, sin_ref,
                     qn_ref, kvin_ref, *, kv_rank):
    mq = mq_ref[...].astype(F32)
    ms = jnp.mean(mq * mq, axis=-1, keepdims=True)
    qn_ref[...] = (mq * lax.rsqrt(ms + EPS) * qw_ref[...]).astype(qn_ref.dtype)
    lat = mkv_ref[:, :kv_rank].astype(F32)
    ms = jnp.mean(lat * lat, axis=-1, keepdims=True)
    kvin_ref[:, :kv_rank] = (lat * lax.rsqrt(ms + EPS) * kvw_ref[...]).astype(kvin_ref.dtype)
    kpe = mkv_ref[:, kv_rank:kv_rank + LANES].astype(F32)
    kvin_ref[:, kv_rank:] = _rope(kpe, cos_ref[...], sin_ref[...]).astype(kvin_ref.dtype)


def _mla_prep(proj, mq_blk, mkv_blk, q_rank, kv_rank, qw, kvw, cos_t, sin_t):
    m = proj.shape[0]
    tm = _pick(m, (256, 128, 64, 32, 16, 8))
    row = lambda width, blk: pl.BlockSpec((tm, width), lambda i: (i, blk))
    full = lambda a: pl.BlockSpec(a.shape, lambda i: (0, 0))
    return pl.pallas_call(
        functools.partial(_mla_prep_kernel, kv_rank=kv_rank),
        grid=(m // tm,),
        in_specs=[row(q_rank, mq_blk), row(q_rank, mkv_blk), full(qw), full(kvw),
                  row(LANES, 0), row(LANES, 0)],
        out_specs=[row(q_rank, 0), row(kv_rank + LANES, 0)],
        out_shape=[jax.ShapeDtypeStruct((m, q_rank), BF16),
                   jax.ShapeDtypeStruct((m, kv_rank + LANES), BF16)],
        compiler_params=_params("parallel"),
        name="mla_prep",
    )(proj, proj, qw, kvw, cos_t, sin_t)


def _postmix_kernel(ohg_ref, g_ref, ofox_ref, omla_ref, hgw_ref, fw_ref, mw_ref,
                    y_ref, *, hg_w, fox_w):
    def normed(o_ref, w_ref):
        o = o_ref[...].astype(F32)
        ms = jnp.mean(o * o, axis=-1, keepdims=True)
        return o * lax.rsqrt(ms + EPS) * w_ref[...]

    g = g_ref[...].astype(F32)
    y_hg = normed(ohg_ref, hgw_ref) * (g * (1.0 / (1.0 + jnp.exp(-g))))
    y_ref[:, :hg_w] = y_hg.astype(y_ref.dtype)
    y_ref[:, hg_w:hg_w + fox_w] = normed(ofox_ref, fw_ref).astype(y_ref.dtype)
    y_ref[:, hg_w + fox_w:] = normed(omla_ref, mw_ref).astype(y_ref.dtype)


def _postmix(o_hg, proj, g_blk, o_fox, o_mla, hgw, fw, mw):
    m, hg_w = o_hg.shape
    fox_w = o_fox.shape[1]
    mla_w = o_mla.shape[1]
    tm = _pick(m, (256, 128, 64, 32, 16, 8))
    row = lambda width, blk: pl.BlockSpec((tm, width), lambda i: (i, blk))
    full = lambda a: pl.BlockSpec(a.shape, lambda i: (0, 0))
    return pl.pallas_call(
        functools.partial(_postmix_kernel, hg_w=hg_w, fox_w=fox_w),
        grid=(m // tm,),
        in_specs=[row(hg_w, 0), row(hg_w, g_blk), row(fox_w, 0), row(mla_w, 0),
                  full(hgw), full(fw), full(mw)],
        out_specs=row(hg_w + fox_w + mla_w, 0),
        out_shape=jax.ShapeDtypeStruct((m, hg_w + fox_w + mla_w), BF16),
        compiler_params=_params("parallel"),
        name="postmix",
    )(o_hg, proj, o_fox, o_mla, hgw, fw, mw)


def kernel(x, positions, mix_norm, w_in, hg_lower_bounds, hg_out_norm, fox_f_bias,
           fox_out_norm, mla_q_a_norm, mla_w_q_b, mla_kv_a_norm, mla_w_kv_b,
           mla_out_norm, w_o, ffn_norm, w_gate, w_up, w_down, final_norm):
    batch, seq, d_model = x.shape
    depth = w_in.shape[0]
    m = batch * seq
    hg_w = hg_lower_bounds.shape[1]
    hg_heads = hg_w // HEAD_DIM
    fox_heads = fox_f_bias.shape[1]
    fox_w = fox_heads * HEAD_DIM
    q_rank = mla_q_a_norm.shape[1]
    kv_rank = mla_kv_a_norm.shape[1]
    mla_heads = mla_w_q_b.shape[2] // (HEAD_DIM + MLA_ROPE)
    mla_w = mla_heads * HEAD_DIM
    assert kv_rank + LANES <= q_rank and fox_heads <= LANES

    off_fox = 4 * hg_w
    off_ff = off_fox + 3 * fox_w
    off_mq = off_ff + fox_heads
    off_mkv = off_mq + q_rank
    fox_scale = float(HEAD_DIM) ** -0.5 * LOG2E
    mla_scale = float(HEAD_DIM + MLA_ROPE) ** -0.5 * LOG2E
    pad_kv = q_rank - (kv_rank + MLA_ROPE)
    col_scale = np.ones((w_in.shape[2],), np.float32)
    col_scale[off_fox:off_fox + fox_w] = fox_scale
    w_in_t = (jnp.transpose(w_in, (0, 2, 1))
              * jnp.asarray(col_scale)[None, :, None]).astype(BF16)
    w_tail = jnp.pad(w_in_t[:, off_mq:], ((0, 0), (0, pad_kv), (0, 0)))
    w_ff = jnp.pad(w_in_t[:, off_ff:off_mq],
                   ((0, 0), (0, LANES - fox_heads), (0, 0)))
    blk_fv = (off_fox + 2 * fox_w) // HEAD_DIM

    wq = mla_w_q_b.reshape(depth, q_rank, mla_heads, HEAD_DIM + MLA_ROPE) * mla_scale
    w_qb = jnp.pad(wq, ((0, 0),) * 3 + ((0, HEAD_DIM - MLA_ROPE),)).reshape(
        depth, q_rank, 2 * mla_w).astype(BF16)
    wkv = mla_w_kv_b.reshape(depth, kv_rank, mla_heads, 2 * HEAD_DIM)
    k_cols = jnp.pad(wkv[..., :HEAD_DIM], ((0, 0),) * 3 + ((0, HEAD_DIM),)).reshape(
        depth, kv_rank, 2 * mla_w)
    v_cols = wkv[..., HEAD_DIM:].reshape(depth, kv_rank, mla_w)
    route = np.zeros((LANES, mla_heads, 2 * HEAD_DIM), np.float32)
    route[:, :, HEAD_DIM:] = np.eye(LANES, dtype=np.float32)[:, None, :]
    route = np.concatenate([route.reshape(LANES, 2 * mla_w),
                            np.zeros((LANES, mla_w), np.float32)], axis=1)
    w_kvb = jnp.concatenate([
        jnp.concatenate([k_cols, v_cols], axis=2),
        jnp.broadcast_to(jnp.asarray(route), (depth, LANES, 3 * mla_w)),
    ], axis=1).astype(BF16)

    lb_soft = jax.nn.softmax(hg_lower_bounds.astype(F32), axis=0)
    lb_all = (jnp.cumsum(lb_soft, axis=0) - lb_soft[0]).reshape(depth, 1, hg_w)
    fox_bias = jnp.pad(fox_f_bias.astype(F32),
                       ((0, 0), (0, LANES - fox_heads))).reshape(depth, 1, LANES)

    inv_freq = ROPE_BASE ** (-jnp.arange(0, MLA_ROPE, 2, dtype=F32) / MLA_ROPE)
    angles = positions.astype(F32).reshape(m, 1) * inv_freq
    cos, sin = jnp.cos(angles), jnp.sin(angles)
    zpad = jnp.zeros((m, LANES - MLA_ROPE), F32)
    cos_t = jnp.concatenate([cos, cos, zpad], axis=1)
    sin_t = jnp.concatenate([-sin, sin, zpad], axis=1)

    hg_lcat = _tri_cat(HG_CHUNK)
    xr = x.reshape(m, d_model).astype(F32)
    mix_gain = mix_norm.astype(F32).reshape(depth, 1, d_model)
    ffn_gain = ffn_norm.astype(F32).reshape(depth, 1, d_model)
    xb, rstd = _xprep(xr, mix_gain[0])

    for l in range(depth):
        proj = _matmul_norm(xb, rstd, w_in_t, l, BF16, 2048, 512, name="in_proj",
                            n_cols=off_ff)
        tail, w_o_l = _matmul_norm(xb, rstd, w_tail, l, BF16, 1024, 512,
                                   name="in_proj_tail", cast_src=w_o)
        ff = _matmul_norm(xb, rstd, w_ff, l, F32, 1024, LANES, name="ff_proj")

        o_hg = _hgrn2(proj, lb_all, l, batch, seq, hg_heads, hg_lcat)

        qcat, kcat = _fox_prep(ff, fox_bias[l], proj, off_fox, batch, seq, fox_heads)
        o_fox = _flash(qcat, kcat, 0, proj, blk_fv, batch, seq, fox_heads)

        qn, kvin = _mla_prep(tail, 0, 1, q_rank, kv_rank,
                             mla_q_a_norm[l].reshape(1, q_rank).astype(F32),
                             mla_kv_a_norm[l].reshape(1, kv_rank).astype(F32),
                             cos_t, sin_t)
        q_mla = _matmul(qn, w_qb, l, BF16, 2048, mla_w, name="mla_q_b")
        kv_mla = _matmul(kvin, w_kvb, l, BF16, 2048, mla_w, name="mla_kv_b")
        o_mla = _flash(q_mla, kv_mla, 0, kv_mla, 2 * mla_heads, batch, seq,
                       mla_heads, rope=(cos_t, sin_t))

        y = _postmix(o_hg, proj, 3, o_fox, o_mla,
                     hg_out_norm[l].reshape(1, hg_w).astype(F32),
                     fox_out_norm[l].reshape(1, fox_w).astype(F32),
                     mla_out_norm[l].reshape(1, mla_w).astype(F32))
        xr, xb, rstd = _matmul_res(y, w_o_l, 0, xr, ffn_gain[l], 1024, 512,
                                   name="out_proj")

        act, w_down_l = _swiglu_up(xb, rstd, w_gate, w_up, w_down, l, 1024, 256)
        xr, xb, rstd = _matmul_res(act, w_down_l, 0, xr, mix_gain[(l + 1) % depth],
                                   512, 512, name="down_proj")

    out = _rmsnorm(xr, final_norm, x.dtype)
    return out.reshape(batch, seq, d_model)
```
